```python
import math
import jax, jax.numpy as jnp
from jax import lax
import numpy as np

D_MODEL = 1024
BATCH = 32
SEQ = 2048
DEPTH = 4
DEC_BATCH = 8
DEC_SEQ = 2048
PAST_LEN = 128

N_MIXERS = 4
HEAD_DIM = 64
GRID_W = 64
Q_BLOCK = 128
ROPE_THETA = 10000.0
NORM_EPS = 1e-6
DA_HEADS = D_MODEL // (2 * HEAD_DIM)
DA_SUBLN_EPS = 1e-5
GQ_HEADS = D_MODEL // HEAD_DIM
GQ_KV_HEADS = 4
GQ_GROUP = GQ_HEADS // GQ_KV_HEADS
NA_HEADS = D_MODEL // HEAD_DIM
NA_WIN_ROWS = 8
NA_WIN_COLS = 16
RW_HEADS = D_MODEL // HEAD_DIM
RW_DECAY_LORA = 64
RW_AAA_LORA = 64
RW_GATE_LORA = 128
RW_GN_EPS = 64e-5
FFN_HIDDEN = 2816
CONV_WIDTH = 3

kernel_name = "hybrid_bidir_encoder_diff_gqa_natten_rwkv7"

F32 = jnp.float32


def _rms_norm(x, g, eps=NORM_EPS):
    xf = x.astype(F32)
    y = xf * lax.rsqrt(jnp.mean(xf * xf, axis=-1, keepdims=True) + eps)
    return (y * g.astype(F32)).astype(x.dtype)


def _rope(x, pos):
    dim = x.shape[-1]
    half = dim // 2
    inv = ROPE_THETA ** (-jnp.arange(half, dtype=F32) / half)
    ang = pos[:, None] * inv[None, :]
    cos = jnp.cos(ang)[None, :, None, :]
    sin = jnp.sin(ang)[None, :, None, :]
    xf = x.astype(F32)
    x1, x2 = xf[..., :half], xf[..., half:]
    return jnp.concatenate([x1 * cos - x2 * sin, x2 * cos + x1 * sin], axis=-1).astype(x.dtype)


def _axial_rope(x):
    S = x.shape[1]
    t = jnp.arange(S)
    row = (t // GRID_W).astype(F32)
    col = (t % GRID_W).astype(F32)
    h = x.shape[-1] // 2
    return jnp.concatenate([_rope(x[..., :h], row), _rope(x[..., h:], col)], axis=-1)


def _lambda_init(layer_idx):
    return 0.8 - 0.6 * math.exp(-0.3 * layer_idx)


def _diff_attention(x, w_qkv, q_norm, k_norm, lq1, lk1, lq2, lk2, subln, w_o, lambda_init):
    B, S, C = x.shape
    qkv = x @ w_qkv
    q, k, v = jnp.split(qkv, 3, axis=-1)
    pos = jnp.arange(S, dtype=F32)
    q = _rope(_rms_norm(q.reshape(B, S, 2 * DA_HEADS, HEAD_DIM), q_norm), pos)
    k = _rope(_rms_norm(k.reshape(B, S, 2 * DA_HEADS, HEAD_DIM), k_norm), pos)
    q = q.reshape(B, S, DA_HEADS, 2, HEAD_DIM)
    k = k.reshape(B, S, DA_HEADS, 2, HEAD_DIM)
    v = v.reshape(B, S, DA_HEADS, 2 * HEAD_DIM)
    lam = (jnp.exp(jnp.sum(lq1 * lk1).astype(F32)) - jnp.exp(jnp.sum(lq2 * lk2).astype(F32)) + lambda_init)
    scale = HEAD_DIM ** -0.5
    nb = S // Q_BLOCK
    q_blocks = q.reshape(B, nb, Q_BLOCK, DA_HEADS, 2, HEAD_DIM).transpose(1, 0, 2, 3, 4, 5)

    def block(q_blk):
        s = jnp.einsum('bqhcd,bkhcd->bhcqk', q_blk, k).astype(F32) * scale
        p = jax.nn.softmax(s, axis=-1)
        p = p[:, :, 0] - lam * p[:, :, 1]
        return jnp.einsum('bhqk,bkhe->bqhe', p.astype(v.dtype), v)

    o = lax.map(block, q_blocks)
    o = o.transpose(1, 0, 2, 3, 4).reshape(B, S, DA_HEADS, 2 * HEAD_DIM)
    o = _rms_norm(o, subln, eps=DA_SUBLN_EPS) * (1.0 - lambda_init)
    return o.reshape(B, S, C) @ w_o


def _gqa_axial(x, w_qkv, q_norm, k_norm, w_o):
    B, S, C = x.shape
    kvw = GQ_KV_HEADS * HEAD_DIM
    qkv = x @ w_qkv
    q = qkv[..., :C].reshape(B, S, GQ_HEADS, HEAD_DIM)
    k = qkv[..., C:C + kvw].reshape(B, S, GQ_KV_HEADS, HEAD_DIM)
    v = qkv[..., C + kvw:].reshape(B, S, GQ_KV_HEADS, HEAD_DIM)
    q = _axial_rope(_rms_norm(q, q_norm))
    k = _axial_rope(_rms_norm(k, k_norm))
    scale = HEAD_DIM ** -0.5
    nb = S // Q_BLOCK
    q_blocks = q.reshape(B, nb, Q_BLOCK, GQ_KV_HEADS, GQ_GROUP, HEAD_DIM).transpose(1, 0, 2, 3, 4, 5)

    def block(q_blk):
        s = jnp.einsum('bqkgd,bskd->bkgqs', q_blk, k).astype(F32) * scale
        p = jax.nn.softmax(s, axis=-1)
        return jnp.einsum('bkgqs,bskd->bqkgd', p.astype(v.dtype), v)

    o = lax.map(block, q_blocks)
    o = o.transpose(1, 0, 2, 3, 4, 5).reshape(B, S, C)
    return o @ w_o


def _neighborhood_attention(x, w_qkv, q_norm, k_norm, rel_bias, w_o):
    B, S, C = x.shape
    rows = S // GRID_W
    wr = min(NA_WIN_ROWS, rows)
    wc = NA_WIN_COLS
    qkv = x @ w_qkv
    q, k, v = jnp.split(qkv, 3, axis=-1)
    q = _rms_norm(q.reshape(B, S, NA_HEADS, HEAD_DIM), q_norm) * (HEAD_DIM ** -0.5)
    k = _rms_norm(k.reshape(B, S, NA_HEADS, HEAD_DIM), k_norm)
    kr = k.reshape(B, rows, GRID_W, NA_HEADS, HEAD_DIM)
    vr = v.reshape(B, rows, GRID_W, NA_HEADS, HEAD_DIM)
    q_rows = q.reshape(B, rows, GRID_W, NA_HEADS, HEAD_DIM).transpose(1, 0, 2, 3, 4)
    r_ids = jnp.arange(rows)
    row_start = jnp.clip(r_ids - wr // 2, 0, rows - wr)
    cols = jnp.arange(GRID_W)
    col_start = jnp.clip(cols - wc // 2, 0, GRID_W - wc)
    col_mask = (cols[None, :] >= col_start[:, None]) & (cols[None, :] < col_start[:, None] + wc)
    dc = jnp.clip(cols[None, :] - cols[:, None] + (wc - 1), 0, 2 * wc - 2)

    def row_block(args):
        q_row, rs, r = args
        kb = lax.dynamic_slice_in_dim(kr, rs, wr, axis=1)
        vb = lax.dynamic_slice_in_dim(vr, rs, wr, axis=1)
        s = jnp.einsum('bqhd,bjkhd->bhqjk', q_row, kb).astype(F32)
        dr = rs + jnp.arange(wr) - r + (NA_WIN_ROWS - 1)
        bias = rel_bias[:, dr[None, :, None], dc[:, None, :]]
        s = s + bias[None].astype(F32)
        s = jnp.where(col_mask[None, None, :, None, :], s, -1e30)
        p = jax.nn.softmax(s.reshape(B, NA_HEADS, GRID_W, wr * GRID_W), axis=-1)
        p = p.reshape(B, NA_HEADS, GRID_W, wr, GRID_W).astype(vb.dtype)
        return jnp.einsum('bhqjk,bjkhd->bqhd', p, vb)

    o = lax.map(row_block, (q_rows, row_start, r_ids))
    o = o.transpose(1, 0, 2, 3, 4).reshape(B, S, C)
    return o @ w_o


def _wkv7_scan(r, w, k, v, a, b, reverse):
    B, S, H, N = r.shape
    xs = tuple(jnp.moveaxis(t, 1, 0) for t in (r, w, k, v, a, b))

    def step(state, inp):
        r_t, w_t, k_t, v_t, a_t, b_t = inp
        sa = jnp.einsum('bhij,bhj->bhi', state, a_t)
        state = (state * w_t[:, :, None, :] + sa[..., None] * b_t[:, :, None, :]
                 + v_t[..., None] * k_t[:, :, None, :])
        return state, jnp.einsum('bhij,bhj->bhi', state, r_t)

    s0 = jnp.zeros((B, H, N, N), F32)
    _, ys = lax.scan(step, s0, xs, reverse=reverse)
    return jnp.moveaxis(ys, 0, 1)


def _rwkv7_time_mix(x, mu, w_r, w_k, w_v, w_o, g1, g2, k_k, k_a, r_k, ln_g, ln_b, w0, w1, w2, a0, a1, a2):
    B, S, C = x.shape
    H, N = RW_HEADS, HEAD_DIM
    xp = jnp.pad(x, ((0, 0), (1, 1), (0, 0)))
    xx = 0.5 * (xp[:, :-2] + xp[:, 2:]) - x
    xr, xw, xk, xv, xa, xg = [x + xx * mu[j] for j in range(6)]
    r = (xr @ w_r).reshape(B, S, H, N).astype(F32)
    k = (xk @ w_k).astype(F32)
    v = (xv @ w_v).reshape(B, S, H, N).astype(F32)
    g = jax.nn.sigmoid(xg @ g1) @ g2
    kk = (k * k_k.astype(F32)).reshape(B, S, H, N)
    kk = kk / jnp.maximum(jnp.sqrt(jnp.sum(kk * kk, axis=-1, keepdims=True)), 1e-12)
    ys = []
    ks = []
    for d in range(2):
        w_log = -jax.nn.softplus(-(w0[d] + jnp.tanh(xw @ w1[d]) @ w2[d]).astype(F32)) - 0.5
        decay = jnp.exp(-jnp.exp(w_log)).reshape(B, S, H, N)
        a = jax.nn.sigmoid((a0[d] + (xa @ a1[d]) @ a2[d]).astype(F32))
        kd = (k * (1.0 + (a - 1.0) * k_a.astype(F32))).reshape(B, S, H, N)
        ah = a.reshape(B, S, H, N)
        ys.append(_wkv7_scan(r, decay, kd, v, -kk, kk * ah, reverse=(d == 1)))
        ks.append(kd)
    y = ys[0] + ys[1]
    mean = jnp.mean(y, axis=-1, keepdims=True)
    var = jnp.mean(jnp.square(y - mean), axis=-1, keepdims=True)
    yn = ((y - mean) * lax.rsqrt(var + RW_GN_EPS)).reshape(B, S, C)
    yn = yn * ln_g.astype(F32) + ln_b.astype(F32)
    k_bonus = 0.5 * (ks[0] + ks[1])
    bonus = jnp.sum(r * k_bonus * r_k.astype(F32), axis=-1, keepdims=True) * v
    out = (yn + bonus.reshape(B, S, C)) * g.astype(F32)
    return out.astype(x.dtype) @ w_o


def _conv_ffn(x, w_in, conv_w, conv_b, w_out):
    u = x @ w_in
    up = jnp.pad(u, ((0, 0), (1, 1), (0, 0)))
    u = up[:, :-2] * conv_w[0] + up[:, 1:-1] * conv_w[1] + up[:, 2:] * conv_w[2] + conv_b
    gate, val = jnp.split(u, 2, axis=-1)
    return (jax.nn.silu(gate) * val) @ w_out


def setup_inputs(seed: int = 0) -> dict:
    key = jax.random.key(seed)
    ks = iter(jax.random.split(key, 128))
    C, d = D_MODEL, HEAD_DIM

    def nrm(shape, scale):
        return jax.random.normal(next(ks), shape, F32) * scale

    def gain(shape):
        return 1.0 + nrm(shape, 0.02)

    p = {}
    p['x_prompt'] = nrm((BATCH, SEQ, C), 1.0)
    p['x_sample'] = nrm((DEC_BATCH, DEC_SEQ, C), 1.0)
    for i in range(DEPTH):
        p[f'n{i}_attn'] = gain((C,))
        p[f'n{i}_ffn'] = gain((C,))
    p['a_w_qkv'] = nrm((C, 3 * C), C ** -0.5)
    p['a_q_norm'] = gain((d,))
    p['a_k_norm'] = gain((d,))
    p['a_lq1'] = nrm((d,), 0.1)
    p['a_lk1'] = nrm((d,), 0.1)
    p['a_lq2'] = nrm((d,), 0.1)
    p['a_lk2'] = nrm((d,), 0.1)
    p['a_subln'] = gain((2 * d,))
    p['a_w_o'] = nrm((C, C), C ** -0.5)
    p['b_w_qkv'] = nrm((C, C + 2 * GQ_KV_HEADS * d), C ** -0.5)
    p['b_q_norm'] = gain((d,))
    p['b_k_norm'] = gain((d,))
    p['b_w_o'] = nrm((C, C), C ** -0.5)
    p['c_w_qkv'] = nrm((C, 3 * C), C ** -0.5)
    p['c_q_norm'] = gain((d,))
    p['c_k_norm'] = gain((d,))
    p['c_rel_bias'] = nrm((NA_HEADS, 2 * NA_WIN_ROWS - 1, 2 * NA_WIN_COLS - 1), 0.1)
    p['c_w_o'] = nrm((C, C), C ** -0.5)
    p['d_mu'] = jax.random.uniform(next(ks), (6, C), F32)
    p['d_w_r'] = nrm((C, C), C ** -0.5)
    p['d_w_k'] = nrm((C, C), C ** -0.5)
    p['d_w_v'] = nrm((C, C), C ** -0.5)
    p['d_w_o'] = nrm((C, C), C ** -0.5)
    p['d_g1'] = nrm((C, RW_GATE_LORA), C ** -0.5)
    p['d_g2'] = nrm((RW_GATE_LORA, C), RW_GATE_LORA ** -0.5)
    p['d_k_k'] = 0.85 + nrm((C,), 0.02)
    p['d_k_a'] = gain((C,))
    p['d_r_k'] = nrm((RW_HEADS, d), 0.1)
    p['d_ln_g'] = gain((C,))
    p['d_ln_b'] = nrm((C,), 0.02)
    p['d_w0'] = jax.random.uniform(next(ks), (2, C), F32, -6.5, -1.5)
    p['d_w1'] = nrm((2, C, RW_DECAY_LORA), C ** -0.5)
    p['d_w2'] = nrm((2, RW_DECAY_LORA, C), 0.1 * RW_DECAY_LORA ** -0.5)
    p['d_a0'] = nrm((2, C), 0.1)
    p['d_a1'] = nrm((2, C, RW_AAA_LORA), C ** -0.5)
    p['d_a2'] = nrm((2, RW_AAA_LORA, C), 0.1 * RW_AAA_LORA ** -0.5)
    for i in range(DEPTH):
        p[f'f{i}_w_in'] = nrm((C, 2 * FFN_HIDDEN), C ** -0.5)
        p[f'f{i}_conv_w'] = nrm((CONV_WIDTH, 2 * FFN_HIDDEN), CONV_WIDTH ** -0.5)
        p[f'f{i}_conv_b'] = nrm((2 * FFN_HIDDEN,), 0.02)
        p[f'f{i}_w_out'] = nrm((FFN_HIDDEN, C), FFN_HIDDEN ** -0.5)
    return p


def reference(x_prompt, x_sample,
              n0_attn, n0_ffn, n1_attn, n1_ffn, n2_attn, n2_ffn, n3_attn, n3_ffn,
              a_w_qkv, a_q_norm, a_k_norm, a_lq1, a_lk1, a_lq2, a_lk2, a_subln, a_w_o,
              b_w_qkv, b_q_norm, b_k_norm, b_w_o,
              c_w_qkv, c_q_norm, c_k_norm, c_rel_bias, c_w_o,
              d_mu, d_w_r, d_w_k, d_w_v, d_w_o, d_g1, d_g2, d_k_k, d_k_a, d_r_k, d_ln_g, d_ln_b,
              d_w0, d_w1, d_w2, d_a0, d_a1, d_a2,
              f0_w_in, f0_conv_w, f0_conv_b, f0_w_out,
              f1_w_in, f1_conv_w, f1_conv_b, f1_w_out,
              f2_w_in, f2_conv_w, f2_conv_b, f2_w_out,
              f3_w_in, f3_conv_w, f3_conv_b, f3_w_out):
    norms = ((n0_attn, n0_ffn), (n1_attn, n1_ffn), (n2_attn, n2_ffn), (n3_attn, n3_ffn))
    ffns = ((f0_w_in, f0_conv_w, f0_conv_b, f0_w_out),
            (f1_w_in, f1_conv_w, f1_conv_b, f1_w_out),
            (f2_w_in, f2_conv_w, f2_conv_b, f2_w_out),
            (f3_w_in, f3_conv_w, f3_conv_b, f3_w_out))
    mix_a = (a_w_qkv, a_q_norm, a_k_norm, a_lq1, a_lk1, a_lq2, a_lk2, a_subln, a_w_o)
    mix_b = (b_w_qkv, b_q_norm, b_k_norm, b_w_o)
    mix_c = (c_w_qkv, c_q_norm, c_k_norm, c_rel_bias, c_w_o)
    mix_d = (d_mu, d_w_r, d_w_k, d_w_v, d_w_o, d_g1, d_g2, d_k_k, d_k_a, d_r_k, d_ln_g, d_ln_b,
             d_w0, d_w1, d_w2, d_a0, d_a1, d_a2)

    def encode(x):
        for i in range(DEPTH):
            h = _rms_norm(x, norms[i][0])
            kind = i % N_MIXERS
            if kind == 0:
                h = _diff_attention(h, *mix_a, lambda_init=_lambda_init(i))
            elif kind == 1:
                h = _gqa_axial(h, *mix_b)
            elif kind == 2:
                h = _neighborhood_attention(h, *mix_c)
            else:
                h = _rwkv7_time_mix(h, *mix_d)
            x = x + h
            x = x + _conv_ffn(_rms_norm(x, norms[i][1]), *ffns[i])
        return x

    y_prompt = encode(x_prompt)
    y_sample = encode(x_sample)
    return (y_prompt, y_sample)
```

```python
import functools
import math

import numpy as np
import jax
import jax.numpy as jnp
from jax import lax
from jax.experimental import pallas as pl
from jax.experimental.pallas import tpu as pltpu

F32 = jnp.float32
BF16 = jnp.bfloat16

C = 1024
HD = 64
LANES = 128
SEQ = 2048
GRID_W = 64
ROPE_THETA = 10000.0
NORM_EPS = 1e-6
DA_HEADS = 8
DA_SUBLN_EPS = 1e-5
GQ_KV = 4
NA_WIN_ROWS = 8
NA_WIN_COLS = 16
NA_QROWS = 4
NA_KROWS = 12
RW_GN_EPS = 64e-5
FFN_HIDDEN = 2816
WKV_CHUNK = 64

COL_CHUNK = 256
VMEM_LIMIT = 56 * 1024 * 1024


def _params(*sem):
    return pltpu.CompilerParams(dimension_semantics=sem, vmem_limit_bytes=VMEM_LIMIT)


def _rms(x, g, eps):
    return x * lax.rsqrt(jnp.mean(x * x, axis=-1, keepdims=True) + eps) * g


def _split(x):
    hi = x.astype(BF16)
    lo = (x - hi.astype(F32)).astype(BF16)
    return hi, lo


def _head_ones(n):
    r = lax.broadcasted_iota(jnp.int32, (n, n), 0)
    c = lax.broadcasted_iota(jnp.int32, (n, n), 1)
    return jnp.where((r // HD) == (c // HD), 1.0, 0.0).astype(BF16)


def _head_sum(x, ones):
    hi, lo = _split(x)
    return (jnp.dot(hi, ones, preferred_element_type=F32)
            + jnp.dot(lo, ones, preferred_element_type=F32))


def _dot(a, b):
    return jnp.dot(a, b, preferred_element_type=F32)


def _dot_nt(a, b):
    return lax.dot_general(a, b, (((1,), (1,)), ((), ())), preferred_element_type=F32)


def _shift_rows(u, prev_row, next_row):
    n = u.shape[0]
    row = lax.broadcasted_iota(jnp.int32, u.shape, 0)
    up = jnp.where(row == 0, prev_row, pltpu.roll(u, 1, 0))
    dn = jnp.where(row == n - 1, next_row, pltpu.roll(u, n - 1, 0))
    return up, dn


def _lo_mask(shape):
    return lax.broadcasted_iota(jnp.int32, shape, len(shape) - 1) < HD


def _stack_heads(x):
    m = _lo_mask(x.shape)
    zero = jnp.zeros_like(x)
    return jnp.concatenate([jnp.where(m, x, zero), jnp.where(m, zero, x)], axis=0)


def _proj_kernel(*refs, n_chunks, n_norm_chunks, rope_half):
    if rope_half:
        x_ref, g_ref, w_ref, hn_ref, cos_ref, sin_ref, o_ref = refs
    else:
        x_ref, g_ref, w_ref, hn_ref, o_ref = refs
    xn = _rms(x_ref[...], g_ref[...], NORM_EPS).astype(BF16)
    ones = _head_ones(COL_CHUNK)
    if rope_half:
        lane = lax.broadcasted_iota(jnp.int32, (1, COL_CHUNK), 1)
        first = (lane % (2 * rope_half)) < rope_half
    for c in range(n_chunks):
        cols = slice(c * COL_CHUNK, (c + 1) * COL_CHUNK)
        y = _dot(xn, w_ref[:, cols])
        if c < n_norm_chunks:
            ms = _head_sum(y * y, ones) * (1.0 / HD)
            y = y * lax.rsqrt(ms + NORM_EPS) * hn_ref[:, cols]
            if rope_half:
                rot = jnp.where(first, pltpu.roll(y, COL_CHUNK - rope_half, 1),
                                pltpu.roll(y, rope_half, 1))
                y = y * cos_ref[...] + rot * sin_ref[...]
        o_ref[:, cols] = y.astype(BF16)


def _proj(x, g, w, hn, n_norm_cols, rope=None, rope_half=0, tm=512):
    t = x.shape[0]
    n = w.shape[1]
    tiles_per_seq = SEQ // tm
    in_specs = [
        pl.BlockSpec((tm, C), lambda i: (i, 0)),
        pl.BlockSpec((1, C), lambda i: (0, 0)),
        pl.BlockSpec((C, n), lambda i: (0, 0)),
        pl.BlockSpec((1, n), lambda i: (0, 0)),
    ]
    args = [x, g.reshape(1, C), w, hn.reshape(1, n)]
    if rope_half:
        spec = pl.BlockSpec((tm, COL_CHUNK), lambda i: (i % tiles_per_seq, 0))
        in_specs += [spec, spec]
        args += [rope[0], rope[1]]
    kern = functools.partial(_proj_kernel, n_chunks=n // COL_CHUNK,
                             n_norm_chunks=n_norm_cols // COL_CHUNK, rope_half=rope_half)
    return pl.pallas_call(
        kern,
        grid=(t // tm,),
        in_specs=in_specs,
        out_specs=pl.BlockSpec((tm, n), lambda i: (i, 0)),
        out_shape=jax.ShapeDtypeStruct((t, n), BF16),
        compiler_params=_params("parallel"),
        name="proj",
    )(*args)


def _rope_tables(pos_first, pos_second, half):
    inv = ROPE_THETA ** (-jnp.arange(half, dtype=F32) / half)

    def one(pos):
        ang = pos[:, None] * inv[None, :]
        cos = jnp.concatenate([jnp.cos(ang), jnp.cos(ang)], axis=-1)
        sin = jnp.concatenate([-jnp.sin(ang), jnp.sin(ang)], axis=-1)
        return cos, sin

    c1, s1 = one(pos_first)
    if pos_second is None:
        cos, sin = c1, s1
    else:
        c2, s2 = one(pos_second)
        cos = jnp.concatenate([c1, c2], axis=-1)
        sin = jnp.concatenate([s1, s2], axis=-1)
    reps = COL_CHUNK // cos.shape[-1]
    return jnp.tile(cos, (1, reps)), jnp.tile(sin, (1, reps))


def _out_kernel(a_ref, w_ref, x_ref, o_ref):
    o_ref[...] = x_ref[...] + _dot(a_ref[...], w_ref[...])


def _out_proj(a, w, x, tm=512):
    t = x.shape[0]
    return pl.pallas_call(
        _out_kernel,
        grid=(t // tm,),
        in_specs=[pl.BlockSpec((tm, C), lambda i: (i, 0)),
                  pl.BlockSpec((C, C), lambda i: (0, 0)),
                  pl.BlockSpec((tm, C), lambda i: (i, 0))],
        out_specs=pl.BlockSpec((tm, C), lambda i: (i, 0)),
        out_shape=jax.ShapeDtypeStruct((t, C), F32),
        compiler_params=_params("parallel"),
        name="out_proj",
    )(a, w, x)


HALO = 8


def _halo_rows(xp_ref, xn_ref, tiles_per_seq):
    i = pl.program_id(0)
    pos = i % tiles_per_seq
    halo = jnp.concatenate([xp_ref[...], xn_ref[...]], axis=0)
    row = lax.broadcasted_iota(jnp.int32, halo.shape, 0)
    has_prev = (pos != 0).astype(jnp.int32)
    has_next = (pos != tiles_per_seq - 1).astype(jnp.int32)
    keep = jnp.where(row < HALO, has_prev, has_next)
    return jnp.where(keep != 0, halo, 0.0)


def _ffn_kernel(x_ref, xp_ref, xn_ref, g_ref, wi_ref, cw_ref, cb_ref, wo_ref, o_ref, *,
                tm, tiles_per_seq):
    x = x_ref[...]
    xe = jnp.concatenate([x, _halo_rows(xp_ref, xn_ref, tiles_per_seq)], axis=0)
    xne = _rms(xe, g_ref[...], NORM_EPS).astype(BF16)

    def conv(cols):
        u = _dot(xne, wi_ref[:, cols])
        um = u[:tm]
        up, dn = _shift_rows(um, u[tm + HALO - 1:tm + HALO], u[tm + HALO:tm + HALO + 1])
        cw = cw_ref[:, cols]
        return up * cw[0:1] + um * cw[1:2] + dn * cw[2:3] + cb_ref[:, cols]

    acc = x
    for c in range(FFN_HIDDEN // COL_CHUNK):
        gate = conv(slice(c * COL_CHUNK, (c + 1) * COL_CHUNK))
        val = conv(slice(FFN_HIDDEN + c * COL_CHUNK, FFN_HIDDEN + (c + 1) * COL_CHUNK))
        hid = (gate * jax.nn.sigmoid(gate) * val).astype(BF16)
        acc = acc + _dot(hid, wo_ref[c * COL_CHUNK:(c + 1) * COL_CHUNK, :])
    o_ref[...] = acc


def _halo_specs(tm, t):
    per = tm // HALO
    last = t // HALO - 1
    return [pl.BlockSpec((HALO, C), lambda i: (jnp.maximum(i * per - 1, 0), 0)),
            pl.BlockSpec((HALO, C), lambda i: (jnp.minimum((i + 1) * per, last), 0))]


def _const_spec(shape):
    return pl.BlockSpec(shape, lambda i: (0,) * len(shape))


def _ffn(x, g, w_in, conv_w, conv_b, w_out, tm=512):
    t = x.shape[0]
    kern = functools.partial(_ffn_kernel, tm=tm, tiles_per_seq=SEQ // tm)
    return pl.pallas_call(
        kern,
        grid=(t // tm,),
        in_specs=[pl.BlockSpec((tm, C), lambda i: (i, 0))] + _halo_specs(tm, t) + [
            _const_spec((1, C)),
            _const_spec((C, 2 * FFN_HIDDEN)),
            _const_spec((3, 2 * FFN_HIDDEN)),
            _const_spec((1, 2 * FFN_HIDDEN)),
            _const_spec((FFN_HIDDEN, C))],
        out_specs=pl.BlockSpec((tm, C), lambda i: (i, 0)),
        out_shape=jax.ShapeDtypeStruct((t, C), F32),
        compiler_params=_params("parallel"),
        name="conv_ffn",
    )(x, x, x, g.reshape(1, C), w_in.astype(BF16), conv_w, conv_b.reshape(1, -1),
      w_out.astype(BF16))


def _diff_attn_kernel(lam_ref, q_ref, k_ref, v_ref, sg_ref, o_ref, *, tq):
    qs = _stack_heads(q_ref[...])
    s = _dot_nt(qs, k_ref[...])
    p = jnp.exp(s - jnp.max(s, axis=-1, keepdims=True))
    inv = 1.0 / jnp.sum(p, axis=-1, keepdims=True)
    pd = p[:tq] * inv[:tq] - p[tq:] * (lam_ref[0] * inv[tq:])
    o = _dot(pd.astype(BF16), v_ref[...])
    o_ref[...] = _rms(o, sg_ref[...], DA_SUBLN_EPS).astype(BF16)


def _diff_attn(qkv, lam, subln_gain, b, tq=256):
    q_blocks = C // LANES
    kern = functools.partial(_diff_attn_kernel, tq=tq)
    return pl.pallas_call(
        kern,
        grid=(b, DA_HEADS, SEQ // tq),
        in_specs=[pl.BlockSpec(memory_space=pltpu.SMEM),
                  pl.BlockSpec((None, tq, LANES), lambda bi, h, qi: (bi, qi, h)),
                  pl.BlockSpec((None, SEQ, LANES), lambda bi, h, qi: (bi, 0, q_blocks + h)),
                  pl.BlockSpec((None, SEQ, LANES), lambda bi, h, qi: (bi, 0, 2 * q_blocks + h)),
                  pl.BlockSpec((1, LANES), lambda bi, h, qi: (0, 0))],
        out_specs=pl.BlockSpec((None, tq, LANES), lambda bi, h, qi: (bi, qi, h)),
        out_shape=jax.ShapeDtypeStruct((b, SEQ, C), BF16),
        compiler_params=_params("parallel", "parallel", "parallel"),
        name="diff_attn",
    )(lam, qkv, qkv, qkv, subln_gain)


def _gqa_kernel(q_ref, k_ref, v_ref, o_ref, *, tq):
    q = q_ref[...]
    qs = jnp.concatenate([_stack_heads(q[:, :LANES]), _stack_heads(q[:, LANES:])], axis=0)
    s = _dot_nt(qs, k_ref[...])
    p = jnp.exp(s - jnp.max(s, axis=-1, keepdims=True))
    inv = 1.0 / jnp.sum(p, axis=-1, keepdims=True)
    o = _dot(p.astype(BF16), v_ref[...]) * inv
    m = _lo_mask((tq, LANES))
    o_ref[...] = jnp.concatenate(
        [jnp.where(m, o[:tq], o[tq:2 * tq]), jnp.where(m, o[2 * tq:3 * tq], o[3 * tq:])],
        axis=-1).astype(BF16)


def _gqa(qkv, b, tq=256):
    kern = functools.partial(_gqa_kernel, tq=tq)
    k0 = C // LANES
    return pl.pallas_call(
        kern,
        grid=(b, GQ_KV, SEQ // tq),
        in_specs=[pl.BlockSpec((None, tq, 2 * LANES), lambda bi, j, qi: (bi, qi, j)),
                  pl.BlockSpec((None, SEQ, LANES), lambda bi, j, qi: (bi, 0, k0 + j)),
                  pl.BlockSpec((None, SEQ, LANES), lambda bi, j, qi: (bi, 0, k0 + GQ_KV + j))],
        out_specs=pl.BlockSpec((None, tq, 2 * LANES), lambda bi, j, qi: (bi, qi, j)),
        out_shape=jax.ShapeDtypeStruct((b, SEQ, C), BF16),
        compiler_params=_params("parallel", "parallel", "parallel"),
        name="gqa_attn",
    )(qkv, qkv, qkv)


NA_TQ = NA_QROWS * GRID_W
NA_TK = NA_KROWS * GRID_W
NA_KBLK = NA_TK // NA_TQ
NA_STEPS = SEQ // NA_TQ


def _na_kernel(q_ref, k0_ref, k1_ref, k2_ref, v0_ref, v1_ref, v2_ref, bias_ref, o_ref):
    qs = _stack_heads(q_ref[...])
    k = jnp.concatenate([k0_ref[...], k1_ref[...], k2_ref[...]], axis=0)
    v = jnp.concatenate([v0_ref[...], v1_ref[...], v2_ref[...]], axis=0)
    s = _dot_nt(qs, k) + bias_ref[...]
    p = jnp.exp(s - jnp.max(s, axis=-1, keepdims=True))
    inv = 1.0 / jnp.sum(p, axis=-1, keepdims=True)
    o = _dot(p.astype(BF16), v) * inv
    o_ref[...] = jnp.where(_lo_mask((NA_TQ, LANES)), o[:NA_TQ], o[NA_TQ:]).astype(BF16)


def _na_key_block(i):
    return jnp.clip(i - 1, 0, NA_STEPS - NA_KBLK)


def _na_pattern(i):
    return jnp.minimum(i, 2) + i // (NA_STEPS - 1)


def _na_attn(qkv, bias, b):
    nblk = C // LANES

    def kv_spec(base, j):
        return pl.BlockSpec((None, NA_TQ, LANES),
                            lambda i, hp, bi: (bi, _na_key_block(i) + j, base + hp))

    return pl.pallas_call(
        _na_kernel,
        grid=(NA_STEPS, nblk, b),
        in_specs=[pl.BlockSpec((None, NA_TQ, LANES), lambda i, hp, bi: (bi, i, hp))]
        + [kv_spec(nblk, j) for j in range(NA_KBLK)]
        + [kv_spec(2 * nblk, j) for j in range(NA_KBLK)]
        + [pl.BlockSpec((None, None, 2 * NA_TQ, NA_TK),
                        lambda i, hp, bi: (_na_pattern(i), hp, 0, 0))],
        out_specs=pl.BlockSpec((None, NA_TQ, LANES), lambda i, hp, bi: (bi, i, hp)),
        out_shape=jax.ShapeDtypeStruct((b, SEQ, C), BF16),
        compiler_params=_params("parallel", "parallel", "parallel"),
        name="na_attn",
    )(qkv, qkv, qkv, qkv, qkv, qkv, qkv, bias)


def _na_bias_table(rel_bias):
    rows = SEQ // GRID_W
    wr, wc = NA_WIN_ROWS, NA_WIN_COLS
    heads = rel_bias.shape[0]
    cols = np.arange(GRID_W)
    col_start = np.clip(cols - wc // 2, 0, GRID_W - wc)
    col_mask = (cols[None, :] >= col_start[:, None]) & (cols[None, :] < col_start[:, None] + wc)
    dc = np.clip(cols[None, :] - cols[:, None] + (wc - 1), 0, 2 * wc - 2)
    onehot = (dc.reshape(-1)[None, :] == np.arange(2 * wc - 1)[:, None]).astype(np.float32)
    tiles = jnp.dot(rel_bias.reshape(heads * (2 * wr - 1), 2 * wc - 1), jnp.asarray(onehot),
                    precision=lax.Precision.HIGHEST)
    tiles = tiles.reshape(heads, 2 * wr - 1, GRID_W, GRID_W)
    tiles = jnp.where(jnp.asarray(col_mask)[None, None], tiles, -1e30)
    tiles = jnp.concatenate([tiles, jnp.full((heads, 1, GRID_W, GRID_W), -1e30, F32)], axis=1)
    n_dr = 2 * wr - 1
    idx = np.full((4, NA_QROWS, NA_KROWS), n_dr, np.int32)
    starts = [(0, 0), (NA_QROWS, 0), (2 * NA_QROWS, NA_QROWS), (rows - NA_QROWS, rows - NA_KROWS)]
    for p, (r0, ks) in enumerate(starts):
        for a in range(NA_QROWS):
            r = r0 + a
            rs = min(max(r - wr // 2, 0), rows - wr)
            for bb in range(NA_KROWS):
                j = ks + bb
                if rs <= j < rs + wr:
                    idx[p, a, bb] = j - r + (wr - 1)
    big = tiles[:, jnp.asarray(idx)]
    big = big.transpose(1, 0, 2, 4, 3, 5)
    return big.reshape(4, heads // 2, 2 * NA_TQ, NA_TK)


def _rwkv_pre_kernel(x_ref, xp_ref, xn_ref, g_ref, mu_ref, wr_ref, wk_ref, wv_ref, g1_ref, g2_ref,
                     w1_ref, w2_ref, a1_ref, a2_ref, w0_ref, a0_ref, kk_ref, ka_ref,
                     r_o, v_o, g_o, kk_o, kd0_o, kd1_o, ag0_o, ag1_o, lw0_o, lw1_o, *,
                     tiles_per_seq):
    g = g_ref[...]
    h = _rms(x_ref[...], g, NORM_EPS)
    hh = _rms(_halo_rows(xp_ref, xn_ref, tiles_per_seq), g, NORM_EPS)
    up, dn = _shift_rows(h, hh[HALO - 1:HALO], hh[HALO:HALO + 1])
    xx = 0.5 * (up + dn) - h

    def mix(j):
        return (h + xx * mu_ref[j:j + 1, :]).astype(BF16)

    r_o[...] = _dot(mix(0), wr_ref[...]).astype(BF16)
    v_o[...] = _dot(mix(3), wv_ref[...]).astype(BF16)
    gate = jax.nn.sigmoid(_dot(mix(5), g1_ref[...]))
    g_o[...] = _dot(gate.astype(BF16), g2_ref[...]).astype(BF16)

    k = _dot(mix(2), wk_ref[...])
    kkv = k * kk_ref[...]
    nrm = jnp.sqrt(_head_sum(kkv * kkv, _head_ones(C)))
    kk_o[...] = (kkv / jnp.maximum(nrm, 1e-12)).astype(BF16)

    tw = jnp.tanh(_dot(mix(1), w1_ref[...]))
    ta = _dot(mix(4), a1_ref[...])
    lo = _lo_mask(tw.shape)
    ka = ka_ref[...]
    for d, (kd_o, ag_o, lw_o) in enumerate(((kd0_o, ag0_o, lw0_o), (kd1_o, ag1_o, lw1_o))):
        sel = lo if d == 0 else jnp.logical_not(lo)
        z = w0_ref[d:d + 1, :] + _dot(jnp.where(sel, tw, 0.0).astype(BF16), w2_ref[...])
        softplus = jnp.maximum(-z, 0.0) + jnp.log1p(jnp.exp(-jnp.abs(z)))
        lw_o[...] = -jnp.exp(-softplus - 0.5)
        a = jax.nn.sigmoid(a0_ref[d:d + 1, :]
                           + _dot(jnp.where(sel, ta, 0.0).astype(BF16), a2_ref[...]))
        ag_o[...] = a.astype(BF16)
        kd_o[...] = (k * (1.0 + (a - 1.0) * ka)).astype(BF16)


def _rwkv_pre(x, g, mu, w_r, w_k, w_v, g1, g2, k_k, k_a, w0, w1, w2, a0, a1, a2, tm=256):
    t = x.shape[0]
    lora = w1.shape[-1]
    w1c = jnp.concatenate([w1[0], w1[1]], axis=1).astype(BF16)
    a1c = jnp.concatenate([a1[0], a1[1]], axis=1).astype(BF16)
    w2c = w2.reshape(2 * lora, C).astype(BF16)
    a2c = a2.reshape(2 * lora, C).astype(BF16)
    assert 2 * lora == LANES
    glora = g1.shape[-1]
    row = pl.BlockSpec((tm, C), lambda i: (i, 0))
    kern = functools.partial(_rwkv_pre_kernel, tiles_per_seq=SEQ // tm)
    bf = jax.ShapeDtypeStruct((t, C), BF16)
    f32 = jax.ShapeDtypeStruct((t, C), F32)
    return pl.pallas_call(
        kern,
        grid=(t // tm,),
        in_specs=[row] + _halo_specs(tm, t) + [
            _const_spec((1, C)), _const_spec((6, C)),
            _const_spec((C, C)), _const_spec((C, C)), _const_spec((C, C)),
            _const_spec((C, glora)), _const_spec((glora, C)),
            _const_spec((C, LANES)), _const_spec((LANES, C)),
            _const_spec((C, LANES)), _const_spec((LANES, C)),
            _const_spec((2, C)), _const_spec((2, C)), _const_spec((1, C)), _const_spec((1, C))],
        out_specs=[row] * 10,
        out_shape=[bf] * 8 + [f32] * 2,
        compiler_params=_params("parallel"),
        name="rwkv_pre",
    )(x, x, x, g.reshape(1, C), mu, w_r.astype(BF16), w_k.astype(BF16), w_v.astype(BF16),
      g1.astype(BF16), g2.astype(BF16), w1c, w2c, a1c, a2c, w0, a0,
      k_k.reshape(1, C), k_a.reshape(1, C))


def _mm3(a, b):
    ah, al = _split(a)
    bh, bl = _split(b)
    return _dot(ah, bh) + _dot(ah, bl) + _dot(al, bh)


def _wkv_kernel(r_ref, v_ref, kk_ref, kd0_ref, kd1_ref, ag0_ref, ag1_ref, lw0_ref, lw1_ref,
                y_ref, *, chunk, n_chunks):
    n2 = 2 * chunk
    ri = lax.broadcasted_iota(jnp.int32, (n2, n2), 0)
    ci = lax.broadcasted_iota(jnp.int32, (n2, n2), 1)
    same_head = (ri // chunk) == (ci // chunk)
    tr, tc = ri % chunk, ci % chunk
    eye = jnp.where(ri == ci, 1.0, 0.0)
    li = lax.broadcasted_iota(jnp.int32, (chunk, chunk), 0)
    lj = lax.broadcasted_iota(jnp.int32, (chunk, chunk), 1)
    n_doublings = int(math.log2(chunk)) - 1

    for d in range(2):
        kd_ref, ag_ref, lw_ref = ((kd0_ref, ag0_ref, lw0_ref), (kd1_ref, ag1_ref, lw1_ref))[d]
        if d == 0:
            strict = same_head & (tr > tc)
            incl = same_head & (tr >= tc)
            tri = jnp.where(li >= lj, 1.0, 0.0).astype(BF16)
        else:
            strict = same_head & (tr < tc)
            incl = same_head & (tr <= tc)
            tri = jnp.where(li <= lj, 1.0, 0.0).astype(BF16)

        def body(step, state, d=d, kd_ref=kd_ref, ag_ref=ag_ref, lw_ref=lw_ref,
                 strict=strict, incl=incl, tri=tri):
            c = step if d == 0 else n_chunks - 1 - step
            rows = pl.ds(pl.multiple_of(c * chunk, chunk), chunk)
            lw = lw_ref[rows, :]
            lw_hi, lw_lo = _split(lw)
            cum = _dot(tri, lw_hi) + _dot(tri, lw_lo)
            tot = jnp.sum(lw, axis=0, keepdims=True)
            kk = kk_ref[rows, :].astype(F32)
            kd = kd_ref[rows, :].astype(F32)
            b = kk * ag_ref[rows, :].astype(F32)
            e_neg = jnp.exp(-cum)
            e_end = jnp.exp(tot - cum)
            a_t = _stack_heads(-kk * jnp.exp(cum - lw))
            r_t = _stack_heads(r_ref[rows, :].astype(F32) * jnp.exp(cum))
            b_t = b * e_neg
            k_t = kd * e_neg
            lhs = jnp.concatenate([a_t, r_t], axis=0).astype(BF16)
            rhs = jnp.concatenate([b_t, b_t, k_t, k_t], axis=0).astype(BF16)
            gram = _dot_nt(lhs, rhs)
            a_ab = jnp.where(strict, gram[:n2, :n2], 0.0)
            a_ak = jnp.where(strict, gram[:n2, n2:], 0.0)
            m_rb = jnp.where(incl, gram[n2:, :n2], 0.0)
            m_rk = jnp.where(incl, gram[n2:, n2:], 0.0)
            inv = eye + a_ab
            pw = a_ab
            for _ in range(n_doublings):
                pw = _mm3(pw, pw)
                inv = inv + _mm3(inv, pw)
            v_s = _stack_heads(v_ref[rows, :]).astype(BF16)
            s_bf = state.astype(BF16)
            x = _dot_nt(a_t.astype(BF16), s_bf) + _dot(a_ak.astype(BF16), v_s)
            u = _mm3(inv, x)
            u_bf = u.astype(BF16)
            y = (_dot_nt(r_t.astype(BF16), s_bf) + _dot(m_rb.astype(BF16), u_bf)
                 + _dot(m_rk.astype(BF16), v_s))
            y = y[:chunk] + y[chunk:]
            if d == 0:
                y_ref[rows, :] = y
            else:
                y_ref[rows, :] = y_ref[rows, :] + y
            uv = jnp.concatenate([u, v_s.astype(F32)], axis=0)
            bk = jnp.concatenate([_stack_heads(b * e_end), _stack_heads(kd * e_end)], axis=0)
            return state * jnp.exp(tot) + _dot(uv.T.astype(BF16), bk.astype(BF16))

        lax.fori_loop(0, n_chunks, body, jnp.zeros((LANES, LANES), F32))


def _wkv(r, v, kk, kd0, kd1, ag0, ag1, lw0, lw1, b):
    spec = pl.BlockSpec((None, SEQ, LANES), lambda bi, hp: (bi, 0, hp))
    args = [a.reshape(b, SEQ, C) for a in (r, v, kk, kd0, kd1, ag0, ag1, lw0, lw1)]
    kern = functools.partial(_wkv_kernel, chunk=WKV_CHUNK, n_chunks=SEQ // WKV_CHUNK)
    return pl.pallas_call(
        kern,
        grid=(b, C // LANES),
        in_specs=[spec] * 9,
        out_specs=spec,
        out_shape=jax.ShapeDtypeStruct((b, SEQ, C), F32),
        compiler_params=_params("parallel", "parallel"),
        name="wkv7",
    )(*args)


def _rwkv_post_kernel(y_ref, r_ref, kd0_ref, kd1_ref, v_ref, g_ref, x_ref, lng_ref, lnb_ref,
                      rk_ref, wo_ref, o_ref):
    ones = _head_ones(C)
    y = y_ref[...]
    dev = y - _head_sum(y, ones) * (1.0 / HD)
    var = _head_sum(dev * dev, ones) * (1.0 / HD)
    yn = dev * lax.rsqrt(var + RW_GN_EPS) * lng_ref[...] + lnb_ref[...]
    k_bonus = 0.5 * (kd0_ref[...].astype(F32) + kd1_ref[...].astype(F32))
    bonus = _head_sum(r_ref[...].astype(F32) * k_bonus * rk_ref[...], ones) * v_ref[...].astype(F32)
    out = ((yn + bonus) * g_ref[...].astype(F32)).astype(BF16)
    o_ref[...] = x_ref[...] + _dot(out, wo_ref[...])


def _rwkv_post(y, r, kd0, kd1, v, g, x, ln_g, ln_b, r_k, w_o, tm=256):
    t = x.shape[0]
    row = pl.BlockSpec((tm, C), lambda i: (i, 0))
    vec = _const_spec((1, C))
    return pl.pallas_call(
        _rwkv_post_kernel,
        grid=(t // tm,),
        in_specs=[row] * 7 + [vec, vec, vec, _const_spec((C, C))],
        out_specs=row,
        out_shape=jax.ShapeDtypeStruct((t, C), F32),
        compiler_params=_params("parallel"),
        name="rwkv_post",
    )(y, r, kd0, kd1, v, g, x, ln_g.reshape(1, C), ln_b.reshape(1, C), r_k.reshape(1, C),
      w_o.astype(BF16))


def _lambda_init(layer_idx):
    return 0.8 - 0.6 * math.exp(-0.3 * layer_idx)


def _head_gain(q_gain, n_q, k_gain, n_k, n_v):
    scale = HD ** -0.5
    return jnp.concatenate([jnp.tile(q_gain * scale, n_q // HD), jnp.tile(k_gain, n_k // HD),
                            jnp.ones((n_v,), F32)])


def _encode(x, b, p):
    t = b * SEQ
    x = x.reshape(t, C)
    pos = jnp.arange(SEQ, dtype=F32)

    hn = _head_gain(p['a_q_norm'], C, p['a_k_norm'], C, C)
    qkv = _proj(x, p['n0_attn'], p['a_w_qkv'].astype(BF16), hn, 2 * C,
                rope=_rope_tables(pos, None, HD // 2), rope_half=HD // 2)
    lam_init = _lambda_init(0)
    lam = (jnp.exp(jnp.sum(p['a_lq1'] * p['a_lk1'])) - jnp.exp(jnp.sum(p['a_lq2'] * p['a_lk2']))
           + lam_init).reshape(1)
    o = _diff_attn(qkv.reshape(b, SEQ, 3 * C), lam,
                   (p['a_subln'] * (1.0 - lam_init)).reshape(1, LANES), b)
    x = _out_proj(o.reshape(t, C), p['a_w_o'].astype(BF16), x)
    x = _ffn(x, p['n0_ffn'], p['f0_w_in'], p['f0_conv_w'], p['f0_conv_b'], p['f0_w_out'])

    kvw = GQ_KV * HD
    w = p['b_w_qkv']

    def dup(wc):
        wc = wc.reshape(C, GQ_KV, 1, HD)
        return jnp.broadcast_to(wc, (C, GQ_KV, 2, HD)).reshape(C, 2 * kvw)

    w_dup = jnp.concatenate([w[:, :C], dup(w[:, C:C + kvw]), dup(w[:, C + kvw:])], axis=1)
    hn = _head_gain(p['b_q_norm'], C, p['b_k_norm'], 2 * kvw, 2 * kvw)
    tok = jnp.arange(SEQ)
    rope = _rope_tables((tok // GRID_W).astype(F32), (tok % GRID_W).astype(F32), HD // 4)
    qkv = _proj(x, p['n1_attn'], w_dup.astype(BF16), hn, C + 2 * kvw, rope=rope,
                rope_half=HD // 4)
    o = _gqa(qkv.reshape(b, SEQ, C + 4 * kvw), b)
    x = _out_proj(o.reshape(t, C), p['b_w_o'].astype(BF16), x)
    x = _ffn(x, p['n1_ffn'], p['f1_w_in'], p['f1_conv_w'], p['f1_conv_b'], p['f1_w_out'])

    hn = _head_gain(p['c_q_norm'], C, p['c_k_norm'], C, C)
    qkv = _proj(x, p['n2_attn'], p['c_w_qkv'].astype(BF16), hn, 2 * C)
    o = _na_attn(qkv.reshape(b, SEQ, 3 * C), _na_bias_table(p['c_rel_bias']), b)
    x = _out_proj(o.reshape(t, C), p['c_w_o'].astype(BF16), x)
    x = _ffn(x, p['n2_ffn'], p['f2_w_in'], p['f2_conv_w'], p['f2_conv_b'], p['f2_w_out'])

    r, v, g, kk, kd0, kd1, ag0, ag1, lw0, lw1 = _rwkv_pre(
        x, p['n3_attn'], p['d_mu'], p['d_w_r'], p['d_w_k'], p['d_w_v'], p['d_g1'], p['d_g2'],
        p['d_k_k'], p['d_k_a'], p['d_w0'], p['d_w1'], p['d_w2'], p['d_a0'], p['d_a1'], p['d_a2'])
    y = _wkv(r, v, kk, kd0, kd1, ag0, ag1, lw0, lw1, b)
    x = _rwkv_post(y.reshape(t, C), r, kd0, kd1, v, g, x, p['d_ln_g'], p['d_ln_b'], p['d_r_k'],
                   p['d_w_o'])
    x = _ffn(x, p['n3_ffn'], p['f3_w_in'], p['f3_conv_w'], p['f3_conv_b'], p['f3_w_out'])
    return x.reshape(b, SEQ, C)


def kernel(x_prompt, x_sample,
           n0_attn, n0_ffn, n1_attn, n1_ffn, n2_attn, n2_ffn, n3_attn, n3_ffn, a_w_qkv,
           a_q_norm, a_k_norm, a_lq1, a_lk1, a_lq2, a_lk2, a_subln, a_w_o, b_w_qkv,
           b_q_norm, b_k_norm, b_w_o, c_w_qkv, c_q_norm, c_k_norm, c_rel_bias, c_w_o, d_mu,
           d_w_r, d_w_k, d_w_v, d_w_o, d_g1, d_g2, d_k_k, d_k_a, d_r_k, d_ln_g, d_ln_b,
           d_w0, d_w1, d_w2, d_a0, d_a1, d_a2, f0_w_in, f0_conv_w, f0_conv_b, f0_w_out,
           f1_w_in, f1_conv_w, f1_conv_b, f1_w_out, f2_w_in, f2_conv_w, f2_conv_b, f2_w_out,
           f3_w_in, f3_conv_w, f3_conv_b, f3_w_out):
    p = dict(locals())
    nb = x_prompt.shape[0]
    x = jnp.concatenate([p.pop('x_prompt'), p.pop('x_sample')], axis=0)
    y = _encode(x, x.shape[0], p)
    return (y[:nb], y[nb:])
```

```python
import functools
import math

import numpy as np
import jax
import jax.numpy as jnp
from jax import lax
from jax.experimental import pallas as pl
from jax.experimental.pallas import tpu as pltpu

F32 = jnp.float32
BF16 = jnp.bfloat16

C = 1024
HD = 64
LANES = 128
SEQ = 2048
GRID_W = 64
ROPE_THETA = 10000.0
NORM_EPS = 1e-6
DA_HEADS = 8
DA_SUBLN_EPS = 1e-5
GQ_KV = 4
NA_WIN_ROWS = 8
NA_WIN_COLS = 16
NA_QROWS = 4
NA_KROWS = 12
RW_GN_EPS = 64e-5
FFN_HIDDEN = 2816
WKV_CHUNK = 64

COL_CHUNK = 256
VMEM_LIMIT = 56 * 1024 * 1024


def _params(*sem):
    return pltpu.CompilerParams(dimension_semantics=sem, vmem_limit_bytes=VMEM_LIMIT)


def _rms(x, g, eps):
    return x * lax.rsqrt(jnp.mean(x * x, axis=-1, keepdims=True) + eps) * g


def _split(x):
    hi = x.astype(BF16)
    lo = (x - hi.astype(F32)).astype(BF16)
    return hi, lo


def _head_ones(n):
    r = lax.broadcasted_iota(jnp.int32, (n, n), 0)
    c = lax.broadcasted_iota(jnp.int32, (n, n), 1)
    return jnp.where((r // HD) == (c // HD), 1.0, 0.0).astype(BF16)


def _head_sum(x, ones):
    hi, lo = _split(x)
    return (jnp.dot(hi, ones, preferred_element_type=F32)
            + jnp.dot(lo, ones, preferred_element_type=F32))


def _dot(a, b):
    return jnp.dot(a, b, preferred_element_type=F32)


def _dot_nt(a, b):
    return lax.dot_general(a, b, (((1,), (1,)), ((), ())), preferred_element_type=F32)


def _shift_rows(u, prev_row, next_row):
    n = u.shape[0]
    row = lax.broadcasted_iota(jnp.int32, u.shape, 0)
    up = jnp.where(row == 0, prev_row, pltpu.roll(u, 1, 0))
    dn = jnp.where(row == n - 1, next_row, pltpu.roll(u, n - 1, 0))
    return up, dn


def _lo_mask(shape):
    return lax.broadcasted_iota(jnp.int32, shape, len(shape) - 1) < HD


def _stack_heads(x):
    m = _lo_mask(x.shape)
    zero = jnp.zeros_like(x)
    return jnp.concatenate([jnp.where(m, x, zero), jnp.where(m, zero, x)], axis=-2)


def _proj_kernel(*refs, n_chunks, n_norm_chunks, rope_half):
    if rope_half:
        x_ref, g_ref, w_ref, hn_ref, cos_ref, sin_ref, o_ref = refs
    else:
        x_ref, g_ref, w_ref, hn_ref, o_ref = refs
    xn = _rms(x_ref[...], g_ref[...], NORM_EPS).astype(BF16)
    ones = _head_ones(COL_CHUNK)
    if rope_half:
        lane = lax.broadcasted_iota(jnp.int32, (1, COL_CHUNK), 1)
        first = (lane % (2 * rope_half)) < rope_half
    for c in range(n_chunks):
        cols = slice(c * COL_CHUNK, (c + 1) * COL_CHUNK)
        y = _dot(xn, w_ref[:, cols])
        if c < n_norm_chunks:
            ms = _head_sum(y * y, ones) * (1.0 / HD)
            y = y * lax.rsqrt(ms + NORM_EPS) * hn_ref[:, cols]
            if rope_half:
                rot = jnp.where(first, pltpu.roll(y, COL_CHUNK - rope_half, 1),
                                pltpu.roll(y, rope_half, 1))
                y = y * cos_ref[...] + rot * sin_ref[...]
        o_ref[:, cols] = y.astype(BF16)


def _proj(x, g, w, hn, n_norm_cols, rope=None, rope_half=0, tm=512):
    t = x.shape[0]
    n = w.shape[1]
    tiles_per_seq = SEQ // tm
    in_specs = [
        pl.BlockSpec((tm, C), lambda i: (i, 0)),
        pl.BlockSpec((1, C), lambda i: (0, 0)),
        pl.BlockSpec((C, n), lambda i: (0, 0)),
        pl.BlockSpec((1, n), lambda i: (0, 0)),
    ]
    args = [x, g.reshape(1, C), w, hn.reshape(1, n)]
    if rope_half:
        spec = pl.BlockSpec((tm, COL_CHUNK), lambda i: (i % tiles_per_seq, 0))
        in_specs += [spec, spec]
        args += [rope[0], rope[1]]
    kern = functools.partial(_proj_kernel, n_chunks=n // COL_CHUNK,
                             n_norm_chunks=n_norm_cols // COL_CHUNK, rope_half=rope_half)
    return pl.pallas_call(
        kern,
        grid=(t // tm,),
        in_specs=in_specs,
        out_specs=pl.BlockSpec((tm, n), lambda i: (i, 0)),
        out_shape=jax.ShapeDtypeStruct((t, n), BF16),
        compiler_params=_params("parallel"),
        name="proj",
    )(*args)


def _rope_tables(pos_first, pos_second, half):
    inv = ROPE_THETA ** (-jnp.arange(half, dtype=F32) / half)

    def one(pos):
        ang = pos[:, None] * inv[None, :]
        cos = jnp.concatenate([jnp.cos(ang), jnp.cos(ang)], axis=-1)
        sin = jnp.concatenate([-jnp.sin(ang), jnp.sin(ang)], axis=-1)
        return cos, sin

    c1, s1 = one(pos_first)
    if pos_second is None:
        cos, sin = c1, s1
    else:
        c2, s2 = one(pos_second)
        cos = jnp.concatenate([c1, c2], axis=-1)
        sin = jnp.concatenate([s1, s2], axis=-1)
    reps = COL_CHUNK // cos.shape[-1]
    return jnp.tile(cos, (1, reps)), jnp.tile(sin, (1, reps))


def _out_kernel(a_ref, w_ref, x_ref, o_ref):
    o_ref[...] = x_ref[...] + _dot(a_ref[...], w_ref[...])


def _out_proj(a, w, x, tm=512):
    t = x.shape[0]
    return pl.pallas_call(
        _out_kernel,
        grid=(t // tm,),
        in_specs=[pl.BlockSpec((tm, C), lambda i: (i, 0)),
                  pl.BlockSpec((C, C), lambda i: (0, 0)),
                  pl.BlockSpec((tm, C), lambda i: (i, 0))],
        out_specs=pl.BlockSpec((tm, C), lambda i: (i, 0)),
        out_shape=jax.ShapeDtypeStruct((t, C), F32),
        compiler_params=_params("parallel"),
        name="out_proj",
    )(a, w, x)


HALO = 8


def _halo_rows(xp_ref, xn_ref, tiles_per_seq):
    i = pl.program_id(0)
    pos = i % tiles_per_seq
    halo = jnp.concatenate([xp_ref[...], xn_ref[...]], axis=0)
    row = lax.broadcasted_iota(jnp.int32, halo.shape, 0)
    has_prev = (pos != 0).astype(jnp.int32)
    has_next = (pos != tiles_per_seq - 1).astype(jnp.int32)
    keep = jnp.where(row < HALO, has_prev, has_next)
    return jnp.where(keep != 0, halo, 0.0)


def _ffn_kernel(x_ref, xp_ref, xn_ref, g_ref, wi_ref, cw_ref, cb_ref, wo_ref, o_ref, *,
                tm, tiles_per_seq):
    x = x_ref[...]
    xe = jnp.concatenate([x, _halo_rows(xp_ref, xn_ref, tiles_per_seq)], axis=0)
    xne = _rms(xe, g_ref[...], NORM_EPS).astype(BF16)

    def conv(cols):
        u = _dot(xne, wi_ref[:, cols])
        um = u[:tm]
        up, dn = _shift_rows(um, u[tm + HALO - 1:tm + HALO], u[tm + HALO:tm + HALO + 1])
        cw = cw_ref[:, cols]
        return up * cw[0:1] + um * cw[1:2] + dn * cw[2:3] + cb_ref[:, cols]

    acc = x
    for c in range(FFN_HIDDEN // COL_CHUNK):
        gate = conv(slice(c * COL_CHUNK, (c + 1) * COL_CHUNK))
        val = conv(slice(FFN_HIDDEN + c * COL_CHUNK, FFN_HIDDEN + (c + 1) * COL_CHUNK))
        hid = (gate * jax.nn.sigmoid(gate) * val).astype(BF16)
        acc = acc + _dot(hid, wo_ref[c * COL_CHUNK:(c + 1) * COL_CHUNK, :])
    o_ref[...] = acc


def _halo_specs(tm, t):
    per = tm // HALO
    last = t // HALO - 1
    return [pl.BlockSpec((HALO, C), lambda i: (jnp.maximum(i * per - 1, 0), 0)),
            pl.BlockSpec((HALO, C), lambda i: (jnp.minimum((i + 1) * per, last), 0))]


def _const_spec(shape):
    return pl.BlockSpec(shape, lambda i: (0,) * len(shape))


def _ffn(x, g, w_in, conv_w, conv_b, w_out, tm=512):
    t = x.shape[0]
    kern = functools.partial(_ffn_kernel, tm=tm, tiles_per_seq=SEQ // tm)
    return pl.pallas_call(
        kern,
        grid=(t // tm,),
        in_specs=[pl.BlockSpec((tm, C), lambda i: (i, 0))] + _halo_specs(tm, t) + [
            _const_spec((1, C)),
            _const_spec((C, 2 * FFN_HIDDEN)),
            _const_spec((3, 2 * FFN_HIDDEN)),
            _const_spec((1, 2 * FFN_HIDDEN)),
            _const_spec((FFN_HIDDEN, C))],
        out_specs=pl.BlockSpec((tm, C), lambda i: (i, 0)),
        out_shape=jax.ShapeDtypeStruct((t, C), F32),
        compiler_params=_params("parallel"),
        name="conv_ffn",
    )(x, x, x, g.reshape(1, C), w_in.astype(BF16), conv_w, conv_b.reshape(1, -1),
      w_out.astype(BF16))


ATTN_SUB = 256


def _softmax_rows(s):
    p = jnp.exp(s - jnp.max(s, axis=-1, keepdims=True))
    return p, 1.0 / jnp.sum(p, axis=-1, keepdims=True)


def _diff_attn_kernel(lam_ref, q_ref, k_ref, v_ref, sg_ref, o_ref, *, tq):
    k = k_ref[...]
    v = v_ref[...]
    n = ATTN_SUB // 2
    for t in range(tq // n):
        rows = slice(t * n, (t + 1) * n)
        qs = _stack_heads(q_ref[rows, :])
        p, inv = _softmax_rows(_dot_nt(qs, k))
        o2 = _dot(p.astype(BF16), v) * inv
        o = o2[:n] - lam_ref[0] * o2[n:]
        o_ref[rows, :] = _rms(o, sg_ref[...], DA_SUBLN_EPS).astype(BF16)


def _diff_attn(qkv, lam, subln_gain, b, tq=256):
    q_blocks = C // LANES
    kern = functools.partial(_diff_attn_kernel, tq=tq)
    return pl.pallas_call(
        kern,
        grid=(b, DA_HEADS, SEQ // tq),
        in_specs=[pl.BlockSpec(memory_space=pltpu.SMEM),
                  pl.BlockSpec((None, tq, LANES), lambda bi, h, qi: (bi, qi, h)),
                  pl.BlockSpec((None, SEQ, LANES), lambda bi, h, qi: (bi, 0, q_blocks + h)),
                  pl.BlockSpec((None, SEQ, LANES), lambda bi, h, qi: (bi, 0, 2 * q_blocks + h)),
                  pl.BlockSpec((1, LANES), lambda bi, h, qi: (0, 0))],
        out_specs=pl.BlockSpec((None, tq, LANES), lambda bi, h, qi: (bi, qi, h)),
        out_shape=jax.ShapeDtypeStruct((b, SEQ, C), BF16),
        compiler_params=_params("parallel", "parallel", "parallel"),
        name="diff_attn",
    )(lam, qkv, qkv, qkv, subln_gain)


def _gqa_kernel(q_ref, k_ref, v_ref, o_ref, *, tq):
    k = k_ref[...]
    v = v_ref[...]
    n = ATTN_SUB // 2
    m = _lo_mask((n, LANES))
    for t in range(tq // n):
        rows = slice(t * n, (t + 1) * n)
        for half in range(2):
            cols = slice(half * LANES, (half + 1) * LANES)
            qs = _stack_heads(q_ref[rows, cols])
            p, inv = _softmax_rows(_dot_nt(qs, k))
            o = _dot(p.astype(BF16), v) * inv
            o_ref[rows, cols] = jnp.where(m, o[:n], o[n:]).astype(BF16)


def _gqa(qkv, b, tq=256):
    kern = functools.partial(_gqa_kernel, tq=tq)
    k0 = C // LANES
    return pl.pallas_call(
        kern,
        grid=(b, GQ_KV, SEQ // tq),
        in_specs=[pl.BlockSpec((None, tq, 2 * LANES), lambda bi, j, qi: (bi, qi, j)),
                  pl.BlockSpec((None, SEQ, LANES), lambda bi, j, qi: (bi, 0, k0 + j)),
                  pl.BlockSpec((None, SEQ, LANES), lambda bi, j, qi: (bi, 0, k0 + GQ_KV + j))],
        out_specs=pl.BlockSpec((None, tq, 2 * LANES), lambda bi, j, qi: (bi, qi, j)),
        out_shape=jax.ShapeDtypeStruct((b, SEQ, C), BF16),
        compiler_params=_params("parallel", "parallel", "parallel"),
        name="gqa_attn",
    )(qkv, qkv, qkv)


NA_TQ = NA_QROWS * GRID_W
NA_TK = NA_KROWS * GRID_W
NA_KBLK = NA_TK // NA_TQ
NA_STEPS = SEQ // NA_TQ
NA_BATCH = 8


def _na_kernel(q_ref, k0_ref, k1_ref, k2_ref, v0_ref, v1_ref, v2_ref, bias_ref, o_ref, *, nb):
    n = ATTN_SUB // 2
    m = _lo_mask((n, LANES))
    for bi in range(nb):
        k = jnp.concatenate([k0_ref[bi], k1_ref[bi], k2_ref[bi]], axis=0)
        v = jnp.concatenate([v0_ref[bi], v1_ref[bi], v2_ref[bi]], axis=0)
        for t in range(NA_TQ // n):
            rows = slice(t * n, (t + 1) * n)
            qs = _stack_heads(q_ref[bi, rows, :])
            bias = jnp.concatenate([bias_ref[rows, :],
                                    bias_ref[NA_TQ + t * n:NA_TQ + (t + 1) * n, :]], axis=0)
            p, inv = _softmax_rows(_dot_nt(qs, k) + bias)
            o = _dot(p.astype(BF16), v) * inv
            o_ref[bi, rows, :] = jnp.where(m, o[:n], o[n:]).astype(BF16)


def _na_key_block(i):
    return jnp.clip(i - 1, 0, NA_STEPS - NA_KBLK)


def _na_pattern(i):
    return jnp.minimum(i, 2) + i // (NA_STEPS - 1)


def _na_attn(qkv, bias, b):
    nblk = C // LANES
    nb = math.gcd(b, NA_BATCH)

    def kv_spec(base, j):
        return pl.BlockSpec((nb, NA_TQ, LANES),
                            lambda i, hp, bi: (bi, _na_key_block(i) + j, base + hp))

    return pl.pallas_call(
        functools.partial(_na_kernel, nb=nb),
        grid=(NA_STEPS, nblk, b // nb),
        in_specs=[pl.BlockSpec((nb, NA_TQ, LANES), lambda i, hp, bi: (bi, i, hp))]
        + [kv_spec(nblk, j) for j in range(NA_KBLK)]
        + [kv_spec(2 * nblk, j) for j in range(NA_KBLK)]
        + [pl.BlockSpec((None, None, 2 * NA_TQ, NA_TK),
                        lambda i, hp, bi: (_na_pattern(i), hp, 0, 0))],
        out_specs=pl.BlockSpec((nb, NA_TQ, LANES), lambda i, hp, bi: (bi, i, hp)),
        out_shape=jax.ShapeDtypeStruct((b, SEQ, C), BF16),
        compiler_params=_params("parallel", "parallel", "parallel"),
        name="na_attn",
    )(qkv, qkv, qkv, qkv, qkv, qkv, qkv, bias)


def _na_bias_table(rel_bias):
    rows = SEQ // GRID_W
    wr, wc = NA_WIN_ROWS, NA_WIN_COLS
    heads = rel_bias.shape[0]
    cols = np.arange(GRID_W)
    col_start = np.clip(cols - wc // 2, 0, GRID_W - wc)
    col_mask = (cols[None, :] >= col_start[:, None]) & (cols[None, :] < col_start[:, None] + wc)
    dc = np.clip(cols[None, :] - cols[:, None] + (wc - 1), 0, 2 * wc - 2)
    onehot = (dc.reshape(-1)[None, :] == np.arange(2 * wc - 1)[:, None]).astype(np.float32)
    tiles = jnp.dot(rel_bias.reshape(heads * (2 * wr - 1), 2 * wc - 1), jnp.asarray(onehot),
                    precision=lax.Precision.HIGHEST)
    tiles = tiles.reshape(heads, 2 * wr - 1, GRID_W, GRID_W)
    tiles = jnp.where(jnp.asarray(col_mask)[None, None], tiles, -1e30)
    tiles = jnp.concatenate([tiles, jnp.full((heads, 1, GRID_W, GRID_W), -1e30, F32)], axis=1)
    n_dr = 2 * wr - 1
    idx = np.full((4, NA_QROWS, NA_KROWS), n_dr, np.int32)
    starts = [(0, 0), (NA_QROWS, 0), (2 * NA_QROWS, NA_QROWS), (rows - NA_QROWS, rows - NA_KROWS)]
    for p, (r0, ks) in enumerate(starts):
        for a in range(NA_QROWS):
            r = r0 + a
            rs = min(max(r - wr // 2, 0), rows - wr)
            for bb in range(NA_KROWS):
                j = ks + bb
                if rs <= j < rs + wr:
                    idx[p, a, bb] = j - r + (wr - 1)
    big = tiles[:, jnp.asarray(idx)]
    big = big.transpose(1, 0, 2, 4, 3, 5)
    return big.reshape(4, heads // 2, 2 * NA_TQ, NA_TK)


def _rwkv_pre_kernel(x_ref, xp_ref, xn_ref, g_ref, mu_ref, wr_ref, wk_ref, wv_ref, g1_ref, g2_ref,
                     w1_ref, w2_ref, a1_ref, a2_ref, w0_ref, a0_ref, kk_ref, ka_ref,
                     r_o, v_o, g_o, kk_o, kd0_o, kd1_o, ag0_o, ag1_o, lw0_o, lw1_o, *,
                     tiles_per_seq):
    g = g_ref[...]
    h = _rms(x_ref[...], g, NORM_EPS)
    hh = _rms(_halo_rows(xp_ref, xn_ref, tiles_per_seq), g, NORM_EPS)
    up, dn = _shift_rows(h, hh[HALO - 1:HALO], hh[HALO:HALO + 1])
    xx = 0.5 * (up + dn) - h

    def mix(j):
        return (h + xx * mu_ref[j:j + 1, :]).astype(BF16)

    r_o[...] = _dot(mix(0), wr_ref[...]).astype(BF16)
    v_o[...] = _dot(mix(3), wv_ref[...]).astype(BF16)
    gate = jax.nn.sigmoid(_dot(mix(5), g1_ref[...]))
    g_o[...] = _dot(gate.astype(BF16), g2_ref[...]).astype(BF16)

    k = _dot(mix(2), wk_ref[...])
    kkv = k * kk_ref[...]
    nrm = jnp.sqrt(_head_sum(kkv * kkv, _head_ones(C)))
    kk_o[...] = (kkv / jnp.maximum(nrm, 1e-12)).astype(BF16)

    tw = jnp.tanh(_dot(mix(1), w1_ref[...]))
    ta = _dot(mix(4), a1_ref[...])
    lo = _lo_mask(tw.shape)
    ka = ka_ref[...]
    for d, (kd_o, ag_o, lw_o) in enumerate(((kd0_o, ag0_o, lw0_o), (kd1_o, ag1_o, lw1_o))):
        sel = lo if d == 0 else jnp.logical_not(lo)
        z = w0_ref[d:d + 1, :] + _dot(jnp.where(sel, tw, 0.0).astype(BF16), w2_ref[...])
        softplus = jnp.maximum(-z, 0.0) + jnp.log1p(jnp.exp(-jnp.abs(z)))
        lw_o[...] = -jnp.exp(-softplus - 0.5)
        a = jax.nn.sigmoid(a0_ref[d:d + 1, :]
                           + _dot(jnp.where(sel, ta, 0.0).astype(BF16), a2_ref[...]))
        ag_o[...] = a.astype(BF16)
        kd_o[...] = (k * (1.0 + (a - 1.0) * ka)).astype(BF16)


def _rwkv_pre(x, g, mu, w_r, w_k, w_v, g1, g2, k_k, k_a, w0, w1, w2, a0, a1, a2, tm=256):
    t = x.shape[0]
    lora = w1.shape[-1]
    w1c = jnp.concatenate([w1[0], w1[1]], axis=1).astype(BF16)
    a1c = jnp.concatenate([a1[0], a1[1]], axis=1).astype(BF16)
    w2c = w2.reshape(2 * lora, C).astype(BF16)
    a2c = a2.reshape(2 * lora, C).astype(BF16)
    assert 2 * lora == LANES
    glora = g1.shape[-1]
    row = pl.BlockSpec((tm, C), lambda i: (i, 0))
    kern = functools.partial(_rwkv_pre_kernel, tiles_per_seq=SEQ // tm)
    bf = jax.ShapeDtypeStruct((t, C), BF16)
    f32 = jax.ShapeDtypeStruct((t, C), F32)
    return pl.pallas_call(
        kern,
        grid=(t // tm,),
        in_specs=[row] + _halo_specs(tm, t) + [
            _const_spec((1, C)), _const_spec((6, C)),
            _const_spec((C, C)), _const_spec((C, C)), _const_spec((C, C)),
            _const_spec((C, glora)), _const_spec((glora, C)),
            _const_spec((C, LANES)), _const_spec((LANES, C)),
            _const_spec((C, LANES)), _const_spec((LANES, C)),
            _const_spec((2, C)), _const_spec((2, C)), _const_spec((1, C)), _const_spec((1, C))],
        out_specs=[row] * 10,
        out_shape=[bf] * 8 + [f32] * 2,
        compiler_params=_params("parallel"),
        name="rwkv_pre",
    )(x, x, x, g.reshape(1, C), mu, w_r.astype(BF16), w_k.astype(BF16), w_v.astype(BF16),
      g1.astype(BF16), g2.astype(BF16), w1c, w2c, a1c, a2c, w0, a0,
      k_k.reshape(1, C), k_a.reshape(1, C))


WKV_CUM_ROWS = 256
WKV_GROUP = 8


def _wkv_kernel(r_ref, v_ref, kk_ref, kd0_ref, kd1_ref, ag0_ref, ag1_ref, lw0_ref, lw1_ref,
                y_ref, cum0_ref, cum1_ref, rt0_ref, rt1_ref, mc0_ref, mc1_ref, dl0_ref, dl1_ref,
                dec0_ref, dec1_ref, *, chunk, n_chunks):
    n2 = 2 * chunk
    ri = lax.broadcasted_iota(jnp.int32, (n2, 2 * n2), 0)
    ci = lax.broadcasted_iota(jnp.int32, (n2, 2 * n2), 1)
    same_head = (ri // chunk) == ((ci % n2) // chunk)
    tr, tc = ri % chunk, ci % chunk
    qi = lax.broadcasted_iota(jnp.int32, (n2, n2), 0)
    qj = lax.broadcasted_iota(jnp.int32, (n2, n2), 1)
    eye = jnp.where(qi == qj, 1.0, 0.0)
    level_masks = [(qi // 2) == (qj // 2)]
    size = 2
    while size < chunk:
        level_masks.append(((qi // (2 * size)) == (qj // (2 * size)))
                           & ((qi // size) != (qj // size)))
        size *= 2
    dirs = ((kd0_ref, ag0_ref, lw0_ref, cum0_ref, rt0_ref, mc0_ref, dl0_ref, dec0_ref),
            (kd1_ref, ag1_ref, lw1_ref, cum1_ref, rt1_ref, mc1_ref, dl1_ref, dec1_ref))

    bi = lax.broadcasted_iota(jnp.int32, (WKV_CUM_ROWS, WKV_CUM_ROWS), 0)
    bj = lax.broadcasted_iota(jnp.int32, (WKV_CUM_ROWS, WKV_CUM_ROWS), 1)
    tri = jnp.where(((bi // chunk) == (bj // chunk)) & (bi >= bj), 1.0, 0.0).astype(BF16)
    for blk in range(SEQ // WKV_CUM_ROWS):
        rows = slice(blk * WKV_CUM_ROWS, (blk + 1) * WKV_CUM_ROWS)
        for (_, _, lw_ref, cum_ref, *_rest) in dirs:
            hi, lo = _split(lw_ref[rows, :])
            cum_ref[rows, :] = _dot(tri, hi) + _dot(tri, lo)

    masks = ((same_head & (tr > tc), same_head & (tr >= tc)),
             (same_head & (tr < tc), same_head & (tr <= tc)))

    grp = WKV_GROUP
    grows = grp * chunk

    def bmm(a, b):
        return jnp.einsum('gmk,gkn->gmn', a, b, preferred_element_type=F32)

    def phase1(g, carry):
        rows = pl.ds(pl.multiple_of(g * grows, grows), grows)
        slab = pl.ds(pl.multiple_of(g * grp, grp), grp)

        def load(ref):
            return ref[rows, :].reshape(grp, chunk, LANES)

        r = load(r_ref).astype(F32)
        kk = load(kk_ref).astype(F32)
        v_s = _stack_heads(load(v_ref))
        v_pad = jnp.concatenate([jnp.zeros((grp, n2, LANES), BF16), v_s], axis=2)
        y_loc = []
        for d, (kd_ref, ag_ref, lw_ref, cum_ref, rt_ref, mc_ref, dl_ref, dec_ref) in enumerate(dirs):
            strict, incl = masks[d]
            pre = load(cum_ref)
            tot = pre[:, chunk - 1:chunk, :]
            if d == 0:
                c_in = pre
                c_ex = pre - load(lw_ref)
            else:
                c_ex = tot - pre
                c_in = c_ex + load(lw_ref)
            kd = load(kd_ref).astype(F32)
            b = kk * load(ag_ref).astype(F32)
            e_neg = jnp.exp(-c_in)
            e_end = jnp.exp(tot - c_in)
            a_t = _stack_heads(-kk * jnp.exp(c_ex))
            r_t = _stack_heads(r * jnp.exp(c_in))
            b_t = (b * e_neg).astype(BF16)
            k_t = (kd * e_neg).astype(BF16)
            lhs = jnp.concatenate([a_t, r_t], axis=1).astype(BF16)
            rhs = jnp.concatenate([b_t, b_t, k_t, k_t], axis=1)
            gram = jnp.einsum('gmd,gnd->gmn', lhs, rhs, preferred_element_type=F32)
            a_abk = jnp.where(strict, gram[:, :n2, :], 0.0)
            a_ab = a_abk[:, :, :n2]
            a_ak = a_abk[:, :, n2:].astype(BF16)
            m_rbk = jnp.where(incl, gram[:, n2:, :], 0.0).astype(BF16)
            inv = eye + jnp.where(level_masks[0], a_ab, 0.0)
            for lvl in range(1, len(level_masks)):
                off = jnp.where(level_masks[lvl], a_ab, 0.0).astype(BF16)
                inv_bf = inv.astype(BF16)
                inv = inv + bmm(bmm(inv_bf, off).astype(BF16), inv_bf)
            x0 = jnp.concatenate([a_t.astype(BF16), bmm(a_ak, v_s).astype(BF16)], axis=2)
            wu = bmm(inv.astype(BF16), x0)
            q = jnp.concatenate([wu.astype(BF16), v_pad], axis=1)
            ry = bmm(m_rbk, q)
            rt_ref[slab] = (r_t + ry[:, :, :LANES]).astype(BF16)
            y_loc.append(ry[:, :chunk, LANES:] + ry[:, chunk:, LANES:])
            bk = jnp.concatenate([_stack_heads(b * e_end), _stack_heads(kd * e_end)],
                                 axis=1).astype(BF16)
            md = jnp.einsum('gkm,gkn->gmn', q, bk, preferred_element_type=F32)
            mc_ref[slab] = md[:, :LANES, :].astype(BF16)
            dl_ref[slab] = md[:, LANES:, :]
            dec_ref[slab] = jnp.broadcast_to(jnp.exp(tot), (grp, 8, LANES))
        y_ref[rows, :] = (y_loc[0] + y_loc[1]).reshape(grows, LANES)
        return carry

    lax.fori_loop(0, n_chunks // grp, phase1, 0)

    def phase2(step, states):
        new = []
        for d, (*_ins, rt_ref, mc_ref, dl_ref, dec_ref) in enumerate(dirs):
            c = step if d == 0 else n_chunks - 1 - step
            rows = pl.ds(pl.multiple_of(c * chunk, chunk), chunk)
            s_bf = states[d].astype(BF16)
            y = _dot_nt(rt_ref[c], s_bf)
            y_ref[rows, :] = y_ref[rows, :] + (y[:chunk] + y[chunk:])
            new.append(states[d] * dec_ref[c][0:1, :] + _dot(s_bf, mc_ref[c]) + dl_ref[c])
        return tuple(new)

    zero = jnp.zeros((LANES, LANES), F32)
    lax.fori_loop(0, n_chunks, phase2, (zero, zero))


def _wkv(r, v, kk, kd0, kd1, ag0, ag1, lw0, lw1, b):
    spec = pl.BlockSpec((None, SEQ, LANES), lambda bi, hp: (bi, 0, hp))
    args = [a.reshape(b, SEQ, C) for a in (r, v, kk, kd0, kd1, ag0, ag1, lw0, lw1)]
    n_chunks = SEQ // WKV_CHUNK
    kern = functools.partial(_wkv_kernel, chunk=WKV_CHUNK, n_chunks=n_chunks)
    per_dir = [pltpu.VMEM((SEQ, LANES), F32),
               pltpu.VMEM((n_chunks, LANES, LANES), BF16),
               pltpu.VMEM((n_chunks, LANES, LANES), BF16),
               pltpu.VMEM((n_chunks, LANES, LANES), F32),
               pltpu.VMEM((n_chunks, 8, LANES), F32)]
    scratch = [s for pair in zip(per_dir, per_dir) for s in pair]
    return pl.pallas_call(
        kern,
        grid=(b, C // LANES),
        in_specs=[spec] * 9,
        out_specs=spec,
        out_shape=jax.ShapeDtypeStruct((b, SEQ, C), F32),
        scratch_shapes=scratch,
        compiler_params=_params("parallel", "parallel"),
        name="wkv7",
    )(*args)


def _rwkv_post_kernel(y_ref, r_ref, kd0_ref, kd1_ref, v_ref, g_ref, x_ref, lng_ref, lnb_ref,
                      rk_ref, wo_ref, o_ref):
    ones = _head_ones(C)
    y = y_ref[...]
    dev = y - _head_sum(y, ones) * (1.0 / HD)
    var = _head_sum(dev * dev, ones) * (1.0 / HD)
    yn = dev * lax.rsqrt(var + RW_GN_EPS) * lng_ref[...] + lnb_ref[...]
    k_bonus = 0.5 * (kd0_ref[...].astype(F32) + kd1_ref[...].astype(F32))
    bonus = _head_sum(r_ref[...].astype(F32) * k_bonus * rk_ref[...], ones) * v_ref[...].astype(F32)
    out = ((yn + bonus) * g_ref[...].astype(F32)).astype(BF16)
    o_ref[...] = x_ref[...] + _dot(out, wo_ref[...])


def _rwkv_post(y, r, kd0, kd1, v, g, x, ln_g, ln_b, r_k, w_o, tm=256):
    t = x.shape[0]
    row = pl.BlockSpec((tm, C), lambda i: (i, 0))
    vec = _const_spec((1, C))
    return pl.pallas_call(
        _rwkv_post_kernel,
        grid=(t // tm,),
        in_specs=[row] * 7 + [vec, vec, vec, _const_spec((C, C))],
        out_specs=row,
        out_shape=jax.ShapeDtypeStruct((t, C), F32),
        compiler_params=_params("parallel"),
        name="rwkv_post",
    )(y, r, kd0, kd1, v, g, x, ln_g.reshape(1, C), ln_b.reshape(1, C), r_k.reshape(1, C),
      w_o.astype(BF16))


def _lambda_init(layer_idx):
    return 0.8 - 0.6 * math.exp(-0.3 * layer_idx)


def _head_gain(q_gain, n_q, k_gain, n_k, n_v):
    scale = HD ** -0.5
    return jnp.concatenate([jnp.tile(q_gain * scale, n_q // HD), jnp.tile(k_gain, n_k // HD),
                            jnp.ones((n_v,), F32)])


def _encode(x, b, p):
    t = b * SEQ
    x = x.reshape(t, C)
    pos = jnp.arange(SEQ, dtype=F32)

    hn = _head_gain(p['a_q_norm'], C, p['a_k_norm'], C, C)
    qkv = _proj(x, p['n0_attn'], p['a_w_qkv'].astype(BF16), hn, 2 * C,
                rope=_rope_tables(pos, None, HD // 2), rope_half=HD // 2)
    lam_init = _lambda_init(0)
    lam = (jnp.exp(jnp.sum(p['a_lq1'] * p['a_lk1'])) - jnp.exp(jnp.sum(p['a_lq2'] * p['a_lk2']))
           + lam_init).reshape(1)
    o = _diff_attn(qkv.reshape(b, SEQ, 3 * C), lam,
                   (p['a_subln'] * (1.0 - lam_init)).reshape(1, LANES), b)
    x = _out_proj(o.reshape(t, C), p['a_w_o'].astype(BF16), x)
    x = _ffn(x, p['n0_ffn'], p['f0_w_in'], p['f0_conv_w'], p['f0_conv_b'], p['f0_w_out'])

    kvw = GQ_KV * HD
    w = p['b_w_qkv']

    def dup(wc):
        wc = wc.reshape(C, GQ_KV, 1, HD)
        return jnp.broadcast_to(wc, (C, GQ_KV, 2, HD)).reshape(C, 2 * kvw)

    w_dup = jnp.concatenate([w[:, :C], dup(w[:, C:C + kvw]), dup(w[:, C + kvw:])], axis=1)
    hn = _head_gain(p['b_q_norm'], C, p['b_k_norm'], 2 * kvw, 2 * kvw)
    tok = jnp.arange(SEQ)
    rope = _rope_tables((tok // GRID_W).astype(F32), (tok % GRID_W).astype(F32), HD // 4)
    qkv = _proj(x, p['n1_attn'], w_dup.astype(BF16), hn, C + 2 * kvw, rope=rope,
                rope_half=HD // 4)
    o = _gqa(qkv.reshape(b, SEQ, C + 4 * kvw), b)
    x = _out_proj(o.reshape(t, C), p['b_w_o'].astype(BF16), x)
    x = _ffn(x, p['n1_ffn'], p['f1_w_in'], p['f1_conv_w'], p['f1_conv_b'], p['f1_w_out'])

    hn = _head_gain(p['c_q_norm'], C, p['c_k_norm'], C, C)
    qkv = _proj(x, p['n2_attn'], p['c_w_qkv'].astype(BF16), hn, 2 * C)
    o = _na_attn(qkv.reshape(b, SEQ, 3 * C), _na_bias_table(p['c_rel_bias']), b)
    x = _out_proj(o.reshape(t, C), p['c_w_o'].astype(BF16), x)
    x = _ffn(x, p['n2_ffn'], p['f2_w_in'], p['f2_conv_w'], p['f2_conv_b'], p['f2_w_out'])

    r, v, g, kk, kd0, kd1, ag0, ag1, lw0, lw1 = _rwkv_pre(
        x, p['n3_attn'], p['d_mu'], p['d_w_r'], p['d_w_k'], p['d_w_v'], p['d_g1'], p['d_g2'],
        p['d_k_k'], p['d_k_a'], p['d_w0'], p['d_w1'], p['d_w2'], p['d_a0'], p['d_a1'], p['d_a2'])
    y = _wkv(r, v, kk, kd0, kd1, ag0, ag1, lw0, lw1, b)
    x = _rwkv_post(y.reshape(t, C), r, kd0, kd1, v, g, x, p['d_ln_g'], p['d_ln_b'], p['d_r_k'],
                   p['d_w_o'])
    x = _ffn(x, p['n3_ffn'], p['f3_w_in'], p['f3_conv_w'], p['f3_conv_b'], p['f3_w_out'])
    return x.reshape(b, SEQ, C)


def kernel(x_prompt, x_sample,
           n0_attn, n0_ffn, n1_attn, n1_ffn, n2_attn, n2_ffn, n3_attn, n3_ffn, a_w_qkv,
           a_q_norm, a_k_norm, a_lq1, a_lk1, a_lq2, a_lk2, a_subln, a_w_o, b_w_qkv,
           b_q_norm, b_k_norm, b_w_o, c_w_qkv, c_q_norm, c_k_norm, c_rel_bias, c_w_o, d_mu,
           d_w_r, d_w_k, d_w_v, d_w_o, d_g1, d_g2, d_k_k, d_k_a, d_r_k, d_ln_g, d_ln_b,
           d_w0, d_w1, d_w2, d_a0, d_a1, d_a2, f0_w_in, f0_conv_w, f0_conv_b, f0_w_out,
           f1_w_in, f1_conv_w, f1_conv_b, f1_w_out, f2_w_in, f2_conv_w, f2_conv_b, f2_w_out,
           f3_w_in, f3_conv_w, f3_conv_b, f3_w_out):
    p = dict(locals())
    nb = x_prompt.shape[0]
    x = jnp.concatenate([p.pop('x_prompt'), p.pop('x_sample')], axis=0)
    y = _encode(x, x.shape[0], p)
    return (y[:nb], y[nb:])
```

```python
import functools
import math

import numpy as np
import jax
import jax.numpy as jnp
from jax import lax
from jax.experimental import pallas as pl
from jax.experimental.pallas import tpu as pltpu

F32 = jnp.float32
BF16 = jnp.bfloat16

C = 1024
HD = 64
LANES = 128
SEQ = 2048
GRID_W = 64
ROPE_THETA = 10000.0
NORM_EPS = 1e-6
DA_HEADS = 8
DA_SUBLN_EPS = 1e-5
GQ_KV = 4
NA_WIN_ROWS = 8
NA_WIN_COLS = 16
NA_QROWS = 4
NA_KROWS = 12
RW_GN_EPS = 64e-5
FFN_HIDDEN = 2816
WKV_CHUNK = 64

COL_CHUNK = 256
VMEM_LIMIT = 56 * 1024 * 1024


def _params(*sem):
    return pltpu.CompilerParams(dimension_semantics=sem, vmem_limit_bytes=VMEM_LIMIT)


def _rms(x, g, eps):
    return x * lax.rsqrt(jnp.mean(x * x, axis=-1, keepdims=True) + eps) * g


def _split(x):
    hi = x.astype(BF16)
    lo = (x - hi.astype(F32)).astype(BF16)
    return hi, lo


def _head_ones(n):
    r = lax.broadcasted_iota(jnp.int32, (n, n), 0)
    c = lax.broadcasted_iota(jnp.int32, (n, n), 1)
    return jnp.where((r // HD) == (c // HD), 1.0, 0.0).astype(BF16)


def _head_sum(x, ones):
    w = ones.shape[0]
    out = []
    for c in range(x.shape[1] // w):
        hi, lo = _split(x[:, c * w:(c + 1) * w])
        out.append(_dot(hi, ones) + _dot(lo, ones))
    return out[0] if len(out) == 1 else jnp.concatenate(out, axis=1)


def _dot(a, b):
    return jnp.dot(a, b, preferred_element_type=F32)


def _dot_nt(a, b):
    return lax.dot_general(a, b, (((1,), (1,)), ((), ())), preferred_element_type=F32)


def _shift_rows(u, prev_row, next_row):
    n = u.shape[0]
    up = pltpu.roll(u, 1, 0)
    dn = pltpu.roll(u, n - 1, 0)
    row = lax.broadcasted_iota(jnp.int32, (8, u.shape[1]), 0)
    up = jnp.concatenate([jnp.where(row == 0, prev_row, up[:8]), up[8:]], axis=0)
    dn = jnp.concatenate([dn[:n - 8], jnp.where(row == 7, next_row, dn[n - 8:])], axis=0)
    return up, dn


def _lo_mask(shape):
    return lax.broadcasted_iota(jnp.int32, shape, len(shape) - 1) < HD


def _stack_heads(x):
    m = _lo_mask(x.shape)
    zero = jnp.zeros_like(x)
    return jnp.concatenate([jnp.where(m, x, zero), jnp.where(m, zero, x)], axis=-2)


def _proj_kernel(*refs, n_chunks, n_norm_chunks, rope_half):
    if rope_half:
        x_ref, g_ref, w_ref, hn_ref, cos_ref, sin_ref, o_ref = refs
    else:
        x_ref, g_ref, w_ref, hn_ref, o_ref = refs
    xn = _rms(x_ref[...], g_ref[...], NORM_EPS).astype(BF16)
    ones = _head_ones(COL_CHUNK)
    if rope_half:
        lane = lax.broadcasted_iota(jnp.int32, (1, COL_CHUNK), 1)
        first = (lane % (2 * rope_half)) < rope_half
    def mm(c):
        return _dot(xn, w_ref[:, c * COL_CHUNK:(c + 1) * COL_CHUNK])

    y_next = mm(0)
    for c in range(n_chunks):
        cols = slice(c * COL_CHUNK, (c + 1) * COL_CHUNK)
        y = y_next
        if c + 1 < n_chunks:
            y_next = mm(c + 1)
        if c < n_norm_chunks:
            ms = _dot((y * y).astype(BF16), ones) * (1.0 / HD)
            y = y * lax.rsqrt(ms + NORM_EPS) * hn_ref[:, cols]
            if rope_half:
                rot = jnp.where(first, pltpu.roll(y, COL_CHUNK - rope_half, 1),
                                pltpu.roll(y, rope_half, 1))
                y = y * cos_ref[...] + rot * sin_ref[...]
        o_ref[:, cols] = y.astype(BF16)


def _proj(x, g, w, hn, n_norm_cols, rope=None, rope_half=0, tm=512):
    t = x.shape[0]
    n = w.shape[1]
    tiles_per_seq = SEQ // tm
    in_specs = [
        pl.BlockSpec((tm, C), lambda i: (i, 0)),
        pl.BlockSpec((1, C), lambda i: (0, 0)),
        pl.BlockSpec((C, n), lambda i: (0, 0)),
        pl.BlockSpec((1, n), lambda i: (0, 0)),
    ]
    args = [x, g.reshape(1, C), w, hn.reshape(1, n)]
    if rope_half:
        spec = pl.BlockSpec((tm, COL_CHUNK), lambda i: (i % tiles_per_seq, 0))
        in_specs += [spec, spec]
        args += [rope[0], rope[1]]
    kern = functools.partial(_proj_kernel, n_chunks=n // COL_CHUNK,
                             n_norm_chunks=n_norm_cols // COL_CHUNK, rope_half=rope_half)
    return pl.pallas_call(
        kern,
        grid=(t // tm,),
        in_specs=in_specs,
        out_specs=pl.BlockSpec((tm, n), lambda i: (i, 0)),
        out_shape=jax.ShapeDtypeStruct((t, n), BF16),
        compiler_params=_params("parallel"),
        name="proj",
    )(*args)


def _rope_tables(pos_first, pos_second, half):
    inv = ROPE_THETA ** (-jnp.arange(half, dtype=F32) / half)

    def one(pos):
        ang = pos[:, None] * inv[None, :]
        cos = jnp.concatenate([jnp.cos(ang), jnp.cos(ang)], axis=-1)
        sin = jnp.concatenate([-jnp.sin(ang), jnp.sin(ang)], axis=-1)
        return cos, sin

    c1, s1 = one(pos_first)
    if pos_second is None:
        cos, sin = c1, s1
    else:
        c2, s2 = one(pos_second)
        cos = jnp.concatenate([c1, c2], axis=-1)
        sin = jnp.concatenate([s1, s2], axis=-1)
    reps = COL_CHUNK // cos.shape[-1]
    return jnp.tile(cos, (1, reps)), jnp.tile(sin, (1, reps))


def _out_kernel(a_ref, w_ref, x_ref, o_ref):
    o_ref[...] = x_ref[...] + _dot(a_ref[...], w_ref[...])


def _out_proj(a, w, x, tm=512):
    t = x.shape[0]
    return pl.pallas_call(
        _out_kernel,
        grid=(t // tm,),
        in_specs=[pl.BlockSpec((tm, C), lambda i: (i, 0)),
                  pl.BlockSpec((C, C), lambda i: (0, 0)),
                  pl.BlockSpec((tm, C), lambda i: (i, 0))],
        out_specs=pl.BlockSpec((tm, C), lambda i: (i, 0)),
        out_shape=jax.ShapeDtypeStruct((t, C), F32),
        compiler_params=_params("parallel"),
        name="out_proj",
    )(a, w, x)


HALO = 8


def _halo_rows(xp_ref, xn_ref, tiles_per_seq):
    i = pl.program_id(0)
    pos = i % tiles_per_seq
    halo = jnp.concatenate([xp_ref[...], xn_ref[...]], axis=0)
    row = lax.broadcasted_iota(jnp.int32, halo.shape, 0)
    has_prev = (pos != 0).astype(jnp.int32)
    has_next = (pos != tiles_per_seq - 1).astype(jnp.int32)
    keep = jnp.where(row < HALO, has_prev, has_next)
    return jnp.where(keep != 0, halo, 0.0)


def _ffn_kernel(x_ref, xp_ref, xn_ref, g_ref, wi_ref, cw_ref, cb_ref, wo_ref, o_ref, hid_ref, *,
                tm, tiles_per_seq):
    x = x_ref[...]
    xe = jnp.concatenate([x, _halo_rows(xp_ref, xn_ref, tiles_per_seq)], axis=0)
    xne = _rms(xe, g_ref[...], NORM_EPS).astype(BF16)

    def in_proj(c):
        gate_cols = slice(c * COL_CHUNK, (c + 1) * COL_CHUNK)
        val_cols = slice(FFN_HIDDEN + c * COL_CHUNK, FFN_HIDDEN + (c + 1) * COL_CHUNK)
        return _dot(xne, wi_ref[:, gate_cols]), _dot(xne, wi_ref[:, val_cols])

    def conv(u, cols):
        um = u[:tm]
        up, dn = _shift_rows(um, u[tm + HALO - 1:tm + HALO], u[tm + HALO:tm + HALO + 1])
        cw = cw_ref[:, cols]
        return up * cw[0:1] + um * cw[1:2] + dn * cw[2:3] + cb_ref[:, cols]

    n_chunks = FFN_HIDDEN // COL_CHUNK
    u_next = in_proj(0)
    for c in range(n_chunks):
        ug, uv = u_next
        if c + 1 < n_chunks:
            u_next = in_proj(c + 1)
        cols = slice(c * COL_CHUNK, (c + 1) * COL_CHUNK)
        gate = conv(ug, cols)
        val = conv(uv, slice(FFN_HIDDEN + c * COL_CHUNK, FFN_HIDDEN + (c + 1) * COL_CHUNK))
        hid_ref[:, cols] = (gate * jax.nn.sigmoid(gate) * val).astype(BF16)
    o_ref[...] = x + _dot(hid_ref[...], wo_ref[...])


def _halo_specs(tm, t):
    per = tm // HALO
    last = t // HALO - 1
    return [pl.BlockSpec((HALO, C), lambda i: (jnp.maximum(i * per - 1, 0), 0)),
            pl.BlockSpec((HALO, C), lambda i: (jnp.minimum((i + 1) * per, last), 0))]


def _const_spec(shape):
    return pl.BlockSpec(shape, lambda i: (0,) * len(shape))


def _ffn(x, g, w_in, conv_w, conv_b, w_out, tm=512):
    t = x.shape[0]
    kern = functools.partial(_ffn_kernel, tm=tm, tiles_per_seq=SEQ // tm)
    return pl.pallas_call(
        kern,
        grid=(t // tm,),
        in_specs=[pl.BlockSpec((tm, C), lambda i: (i, 0))] + _halo_specs(tm, t) + [
            _const_spec((1, C)),
            _const_spec((C, 2 * FFN_HIDDEN)),
            _const_spec((3, 2 * FFN_HIDDEN)),
            _const_spec((1, 2 * FFN_HIDDEN)),
            _const_spec((FFN_HIDDEN, C))],
        out_specs=pl.BlockSpec((tm, C), lambda i: (i, 0)),
        out_shape=jax.ShapeDtypeStruct((t, C), F32),
        scratch_shapes=[pltpu.VMEM((tm, FFN_HIDDEN), BF16)],
        compiler_params=_params("parallel"),
        name="conv_ffn",
    )(x, x, x, g.reshape(1, C), w_in.astype(BF16), conv_w, conv_b.reshape(1, -1),
      w_out.astype(BF16))


ATTN_SUB = 256


def _softmax_rows(s):
    p = jnp.exp(s - jnp.max(s, axis=-1, keepdims=True))
    return p, 1.0 / jnp.sum(p, axis=-1, keepdims=True)


def _diff_attn_kernel(lam_ref, q_ref, k_ref, v_ref, sg_ref, o_ref, *, tq):
    k = k_ref[...]
    v = v_ref[...]
    n = ATTN_SUB // 2
    n_tiles = tq // n

    def scores(t):
        return _dot_nt(_stack_heads(q_ref[t * n:(t + 1) * n, :]), k)

    s_next = scores(0)
    for t in range(n_tiles):
        s = s_next
        if t + 1 < n_tiles:
            s_next = scores(t + 1)
        p, inv = _softmax_rows(s)
        o2 = _dot(p.astype(BF16), v) * inv
        o = o2[:n] - lam_ref[0] * o2[n:]
        o_ref[t * n:(t + 1) * n, :] = _rms(o, sg_ref[...], DA_SUBLN_EPS).astype(BF16)


def _diff_attn(qkv, lam, subln_gain, b, tq=512):
    q_blocks = C // LANES
    kern = functools.partial(_diff_attn_kernel, tq=tq)
    return pl.pallas_call(
        kern,
        grid=(b, DA_HEADS, SEQ // tq),
        in_specs=[pl.BlockSpec(memory_space=pltpu.SMEM),
                  pl.BlockSpec((None, tq, LANES), lambda bi, h, qi: (bi, qi, h)),
                  pl.BlockSpec((None, SEQ, LANES), lambda bi, h, qi: (bi, 0, q_blocks + h)),
                  pl.BlockSpec((None, SEQ, LANES), lambda bi, h, qi: (bi, 0, 2 * q_blocks + h)),
                  pl.BlockSpec((1, LANES), lambda bi, h, qi: (0, 0))],
        out_specs=pl.BlockSpec((None, tq, LANES), lambda bi, h, qi: (bi, qi, h)),
        out_shape=jax.ShapeDtypeStruct((b, SEQ, C), BF16),
        compiler_params=_params("parallel", "parallel", "parallel"),
        name="diff_attn",
    )(lam, qkv, qkv, qkv, subln_gain)


def _gqa_kernel(q_ref, k_ref, v_ref, o_ref, *, tq):
    k = k_ref[...]
    v = v_ref[...]
    n = ATTN_SUB // 2
    m = _lo_mask((n, LANES))
    tiles = [(slice(t * n, (t + 1) * n), slice(half * LANES, (half + 1) * LANES))
             for t in range(tq // n) for half in range(2)]

    def scores(tile):
        return _dot_nt(_stack_heads(q_ref[tile]), k)

    s_next = scores(tiles[0])
    for i, tile in enumerate(tiles):
        s = s_next
        if i + 1 < len(tiles):
            s_next = scores(tiles[i + 1])
        p, inv = _softmax_rows(s)
        o = _dot(p.astype(BF16), v) * inv
        o_ref[tile] = jnp.where(m, o[:n], o[n:]).astype(BF16)


def _gqa(qkv, b, tq=512):
    kern = functools.partial(_gqa_kernel, tq=tq)
    k0 = C // LANES
    return pl.pallas_call(
        kern,
        grid=(b, GQ_KV, SEQ // tq),
        in_specs=[pl.BlockSpec((None, tq, 2 * LANES), lambda bi, j, qi: (bi, qi, j)),
                  pl.BlockSpec((None, SEQ, LANES), lambda bi, j, qi: (bi, 0, k0 + j)),
                  pl.BlockSpec((None, SEQ, LANES), lambda bi, j, qi: (bi, 0, k0 + GQ_KV + j))],
        out_specs=pl.BlockSpec((None, tq, 2 * LANES), lambda bi, j, qi: (bi, qi, j)),
        out_shape=jax.ShapeDtypeStruct((b, SEQ, C), BF16),
        compiler_params=_params("parallel", "parallel", "parallel"),
        name="gqa_attn",
    )(qkv, qkv, qkv)


NA_TQ = NA_QROWS * GRID_W
NA_TK = NA_KROWS * GRID_W
NA_KBLK = NA_TK // NA_TQ
NA_STEPS = SEQ // NA_TQ
NA_BATCH = 8


def _na_kernel(q_ref, k0_ref, k1_ref, k2_ref, v0_ref, v1_ref, v2_ref, bias_ref, o_ref, *, nb):
    n = ATTN_SUB // 2
    m = _lo_mask((n, LANES))
    tiles = [(bi, t) for bi in range(nb) for t in range(NA_TQ // n)]

    def scores(tile):
        bi, t = tile
        k = jnp.concatenate([k0_ref[bi], k1_ref[bi], k2_ref[bi]], axis=0)
        return _dot_nt(_stack_heads(q_ref[bi, t * n:(t + 1) * n, :]), k)

    s_next = scores(tiles[0])
    for i, (bi, t) in enumerate(tiles):
        s = s_next
        if i + 1 < len(tiles):
            s_next = scores(tiles[i + 1])
        rows = slice(t * n, (t + 1) * n)
        bias = jnp.concatenate([bias_ref[rows, :],
                                bias_ref[NA_TQ + t * n:NA_TQ + (t + 1) * n, :]], axis=0)
        p, inv = _softmax_rows(s + bias)
        v = jnp.concatenate([v0_ref[bi], v1_ref[bi], v2_ref[bi]], axis=0)
        o = _dot(p.astype(BF16), v) * inv
        o_ref[bi, rows, :] = jnp.where(m, o[:n], o[n:]).astype(BF16)


def _na_key_block(i):
    return jnp.clip(i - 1, 0, NA_STEPS - NA_KBLK)


def _na_pattern(i):
    return jnp.minimum(i, 2) + i // (NA_STEPS - 1)


def _na_attn(qkv, bias, b):
    nblk = C // LANES
    nb = math.gcd(b, NA_BATCH)

    def kv_spec(base, j):
        return pl.BlockSpec((nb, NA_TQ, LANES),
                            lambda i, hp, bi: (bi, _na_key_block(i) + j, base + hp))

    return pl.pallas_call(
        functools.partial(_na_kernel, nb=nb),
        grid=(NA_STEPS, nblk, b // nb),
        in_specs=[pl.BlockSpec((nb, NA_TQ, LANES), lambda i, hp, bi: (bi, i, hp))]
        + [kv_spec(nblk, j) for j in range(NA_KBLK)]
        + [kv_spec(2 * nblk, j) for j in range(NA_KBLK)]
        + [pl.BlockSpec((None, None, 2 * NA_TQ, NA_TK),
                        lambda i, hp, bi: (_na_pattern(i), hp, 0, 0))],
        out_specs=pl.BlockSpec((nb, NA_TQ, LANES), lambda i, hp, bi: (bi, i, hp)),
        out_shape=jax.ShapeDtypeStruct((b, SEQ, C), BF16),
        compiler_params=_params("parallel", "parallel", "parallel"),
        name="na_attn",
    )(qkv, qkv, qkv, qkv, qkv, qkv, qkv, bias)


def _na_bias_table(rel_bias):
    rows = SEQ // GRID_W
    wr, wc = NA_WIN_ROWS, NA_WIN_COLS
    heads = rel_bias.shape[0]
    cols = np.arange(GRID_W)
    col_start = np.clip(cols - wc // 2, 0, GRID_W - wc)
    col_mask = (cols[None, :] >= col_start[:, None]) & (cols[None, :] < col_start[:, None] + wc)
    dc = np.clip(cols[None, :] - cols[:, None] + (wc - 1), 0, 2 * wc - 2)
    onehot = (dc.reshape(-1)[None, :] == np.arange(2 * wc - 1)[:, None]).astype(np.float32)
    tiles = jnp.dot(rel_bias.reshape(heads * (2 * wr - 1), 2 * wc - 1), jnp.asarray(onehot),
                    precision=lax.Precision.HIGHEST)
    tiles = tiles.reshape(heads, 2 * wr - 1, GRID_W, GRID_W)
    tiles = jnp.where(jnp.asarray(col_mask)[None, None], tiles, -1e30)
    tiles = jnp.concatenate([tiles, jnp.full((heads, 1, GRID_W, GRID_W), -1e30, F32)], axis=1)
    n_dr = 2 * wr - 1
    idx = np.full((4, NA_QROWS, NA_KROWS), n_dr, np.int32)
    starts = [(0, 0), (NA_QROWS, 0), (2 * NA_QROWS, NA_QROWS), (rows - NA_QROWS, rows - NA_KROWS)]
    for p, (r0, ks) in enumerate(starts):
        for a in range(NA_QROWS):
            r = r0 + a
            rs = min(max(r - wr // 2, 0), rows - wr)
            for bb in range(NA_KROWS):
                j = ks + bb
                if rs <= j < rs + wr:
                    idx[p, a, bb] = j - r + (wr - 1)
    big = tiles[:, jnp.asarray(idx)]
    big = big.transpose(1, 0, 2, 4, 3, 5)
    return big.reshape(4, heads // 2, 2 * NA_TQ, NA_TK)


def _rwkv_pre_kernel(x_ref, xp_ref, xn_ref, g_ref, mu_ref, wr_ref, wk_ref, wv_ref, g1_ref, g2_ref,
                     w1_ref, w2_ref, a1_ref, a2_ref, w0_ref, a0_ref, kk_ref, ka_ref,
                     r_o, v_o, g_o, kk_o, kd0_o, kd1_o, ag0_o, ag1_o, lw0_o, lw1_o, *,
                     tiles_per_seq):
    g = g_ref[...]
    h = _rms(x_ref[...], g, NORM_EPS)
    hh = _rms(_halo_rows(xp_ref, xn_ref, tiles_per_seq), g, NORM_EPS)
    up, dn = _shift_rows(h, hh[HALO - 1:HALO], hh[HALO:HALO + 1])
    xx = 0.5 * (up + dn) - h

    def mix(j):
        return (h + xx * mu_ref[j:j + 1, :]).astype(BF16)

    tw = jnp.tanh(_dot(mix(1), w1_ref[...]))
    ta = _dot(mix(4), a1_ref[...])
    gate = jax.nn.sigmoid(_dot(mix(5), g1_ref[...]))
    lo = _lo_mask(tw.shape)
    sels = (lo, jnp.logical_not(lo))
    zs = [_dot(jnp.where(s, tw, 0.0).astype(BF16), w2_ref[...]) for s in sels]
    az = [_dot(jnp.where(s, ta, 0.0).astype(BF16), a2_ref[...]) for s in sels]
    k = _dot(mix(2), wk_ref[...])
    r_o[...] = _dot(mix(0), wr_ref[...]).astype(BF16)
    v_o[...] = _dot(mix(3), wv_ref[...]).astype(BF16)
    g_o[...] = _dot(gate.astype(BF16), g2_ref[...]).astype(BF16)

    kkv = k * kk_ref[...]
    nrm = jnp.sqrt(_head_sum(kkv * kkv, _head_ones(COL_CHUNK)))
    kk_o[...] = (kkv / jnp.maximum(nrm, 1e-12)).astype(BF16)
    ka = ka_ref[...]
    for d, (kd_o, ag_o, lw_o) in enumerate(((kd0_o, ag0_o, lw0_o), (kd1_o, ag1_o, lw1_o))):
        lw_o[...] = -math.exp(-0.5) * jax.nn.sigmoid(w0_ref[d:d + 1, :] + zs[d])
        a = jax.nn.sigmoid(a0_ref[d:d + 1, :] + az[d])
        ag_o[...] = a.astype(BF16)
        kd_o[...] = (k * (1.0 + (a - 1.0) * ka)).astype(BF16)


def _rwkv_pre(x, g, mu, w_r, w_k, w_v, g1, g2, k_k, k_a, w0, w1, w2, a0, a1, a2, tm=256):
    t = x.shape[0]
    lora = w1.shape[-1]
    w1c = jnp.concatenate([w1[0], w1[1]], axis=1).astype(BF16)
    a1c = jnp.concatenate([a1[0], a1[1]], axis=1).astype(BF16)
    w2c = w2.reshape(2 * lora, C).astype(BF16)
    a2c = a2.reshape(2 * lora, C).astype(BF16)
    assert 2 * lora == LANES
    glora = g1.shape[-1]
    row = pl.BlockSpec((tm, C), lambda i: (i, 0))
    kern = functools.partial(_rwkv_pre_kernel, tiles_per_seq=SEQ // tm)
    bf = jax.ShapeDtypeStruct((t, C), BF16)
    f32 = jax.ShapeDtypeStruct((t, C), F32)
    return pl.pallas_call(
        kern,
        grid=(t // tm,),
        in_specs=[row] + _halo_specs(tm, t) + [
            _const_spec((1, C)), _const_spec((6, C)),
            _const_spec((C, C)), _const_spec((C, C)), _const_spec((C, C)),
            _const_spec((C, glora)), _const_spec((glora, C)),
            _const_spec((C, LANES)), _const_spec((LANES, C)),
            _const_spec((C, LANES)), _const_spec((LANES, C)),
            _const_spec((2, C)), _const_spec((2, C)), _const_spec((1, C)), _const_spec((1, C))],
        out_specs=[row] * 10,
        out_shape=[bf] * 8 + [f32] * 2,
        compiler_params=_params("parallel"),
        name="rwkv_pre",
    )(x, x, x, g.reshape(1, C), mu, w_r.astype(BF16), w_k.astype(BF16), w_v.astype(BF16),
      g1.astype(BF16), g2.astype(BF16), w1c, w2c, a1c, a2c, w0, a0,
      k_k.reshape(1, C), k_a.reshape(1, C))


WKV_CUM_ROWS = 256
WKV_GROUP = 8


def _wkv_kernel(r_ref, v_ref, kk_ref, kd0_ref, kd1_ref, ag0_ref, ag1_ref, lw0_ref, lw1_ref,
                y_ref, cum0_ref, cum1_ref, rt0_ref, rt1_ref, mc0_ref, mc1_ref, dl0_ref, dl1_ref,
                dec0_ref, dec1_ref, *, chunk, n_chunks):
    n2 = 2 * chunk
    ri = lax.broadcasted_iota(jnp.int32, (n2, 2 * n2), 0)
    ci = lax.broadcasted_iota(jnp.int32, (n2, 2 * n2), 1)
    same_head = (ri // chunk) == ((ci % n2) // chunk)
    tr, tc = ri % chunk, ci % chunk
    qi = lax.broadcasted_iota(jnp.int32, (n2, n2), 0)
    qj = lax.broadcasted_iota(jnp.int32, (n2, n2), 1)
    eye = jnp.where(qi == qj, 1.0, 0.0)
    level_masks = [(qi // 2) == (qj // 2)]
    size = 2
    while size < chunk:
        level_masks.append(((qi // (2 * size)) == (qj // (2 * size)))
                           & ((qi // size) != (qj // size)))
        size *= 2
    dirs = ((kd0_ref, ag0_ref, lw0_ref, cum0_ref, rt0_ref, mc0_ref, dl0_ref, dec0_ref),
            (kd1_ref, ag1_ref, lw1_ref, cum1_ref, rt1_ref, mc1_ref, dl1_ref, dec1_ref))

    bi = lax.broadcasted_iota(jnp.int32, (WKV_CUM_ROWS, WKV_CUM_ROWS), 0)
    bj = lax.broadcasted_iota(jnp.int32, (WKV_CUM_ROWS, WKV_CUM_ROWS), 1)
    tri = jnp.where(((bi // chunk) == (bj // chunk)) & (bi >= bj), 1.0, 0.0).astype(BF16)
    for blk in range(SEQ // WKV_CUM_ROWS):
        rows = slice(blk * WKV_CUM_ROWS, (blk + 1) * WKV_CUM_ROWS)
        for (_, _, lw_ref, cum_ref, *_rest) in dirs:
            hi, lo = _split(lw_ref[rows, :])
            cum_ref[rows, :] = _dot(tri, hi) + _dot(tri, lo)

    masks = ((same_head & (tr > tc), same_head & (tr >= tc)),
             (same_head & (tr < tc), same_head & (tr <= tc)))

    grp = WKV_GROUP
    grows = grp * chunk

    def bmm(a, b):
        return jnp.einsum('gmk,gkn->gmn', a, b, preferred_element_type=F32)

    def phase1(g, carry):
        rows = pl.ds(pl.multiple_of(g * grows, grows), grows)
        slab = pl.ds(pl.multiple_of(g * grp, grp), grp)

        def load(ref):
            return ref[rows, :].reshape(grp, chunk, LANES)

        r = load(r_ref).astype(F32)
        kk = load(kk_ref).astype(F32)
        v_s = _stack_heads(load(v_ref))
        v_pad = jnp.concatenate([jnp.zeros((grp, n2, LANES), BF16), v_s], axis=2)
        y_loc = []
        for d, (kd_ref, ag_ref, lw_ref, cum_ref, rt_ref, mc_ref, dl_ref, dec_ref) in enumerate(dirs):
            strict, incl = masks[d]
            pre = load(cum_ref)
            tot = pre[:, chunk - 1:chunk, :]
            if d == 0:
                c_in = pre
                c_ex = pre - load(lw_ref)
            else:
                c_ex = tot - pre
                c_in = c_ex + load(lw_ref)
            kd = load(kd_ref).astype(F32)
            b = kk * load(ag_ref).astype(F32)
            e_neg = jnp.exp(-c_in)
            e_end = jnp.exp(tot - c_in)
            a_t = _stack_heads(-kk * jnp.exp(c_ex))
            r_t = _stack_heads(r * jnp.exp(c_in))
            b_t = (b * e_neg).astype(BF16)
            k_t = (kd * e_neg).astype(BF16)
            lhs = jnp.concatenate([a_t, r_t], axis=1).astype(BF16)
            rhs = jnp.concatenate([b_t, b_t, k_t, k_t], axis=1)
            gram = jnp.einsum('gmd,gnd->gmn', lhs, rhs, preferred_element_type=F32)
            a_abk = jnp.where(strict, gram[:, :n2, :], 0.0)
            a_ab = a_abk[:, :, :n2]
            a_ak = a_abk[:, :, n2:].astype(BF16)
            m_rbk = jnp.where(incl, gram[:, n2:, :], 0.0).astype(BF16)
            inv = eye + jnp.where(level_masks[0], a_ab, 0.0)
            for lvl in range(1, len(level_masks)):
                off = jnp.where(level_masks[lvl], a_ab, 0.0).astype(BF16)
                inv_bf = inv.astype(BF16)
                inv = inv + bmm(bmm(inv_bf, off).astype(BF16), inv_bf)
            x0 = jnp.concatenate([a_t.astype(BF16), bmm(a_ak, v_s).astype(BF16)], axis=2)
            wu = bmm(inv.astype(BF16), x0)
            q = jnp.concatenate([wu.astype(BF16), v_pad], axis=1)
            ry = bmm(m_rbk, q)
            rt_ref[slab] = (r_t + ry[:, :, :LANES]).astype(BF16)
            y_loc.append(ry[:, :chunk, LANES:] + ry[:, chunk:, LANES:])
            bk = jnp.concatenate([_stack_heads(b * e_end), _stack_heads(kd * e_end)],
                                 axis=1).astype(BF16)
            md = jnp.einsum('gkm,gkn->gmn', q, bk, preferred_element_type=F32)
            mc_ref[slab] = md[:, :LANES, :].astype(BF16)
            dl_ref[slab] = md[:, LANES:, :]
            dec_ref[slab] = jnp.broadcast_to(jnp.exp(tot), (grp, 8, LANES))
        y_ref[rows, :] = (y_loc[0] + y_loc[1]).reshape(grows, LANES)
        return carry

    lax.fori_loop(0, n_chunks // grp, phase1, 0)

    def phase2(step, states):
        new = []
        for d, (*_ins, rt_ref, mc_ref, dl_ref, dec_ref) in enumerate(dirs):
            c = step if d == 0 else n_chunks - 1 - step
            rows = pl.ds(pl.multiple_of(c * chunk, chunk), chunk)
            s_bf = states[d].astype(BF16)
            y = _dot_nt(rt_ref[c], s_bf)
            y_ref[rows, :] = y_ref[rows, :] + (y[:chunk] + y[chunk:])
            new.append(states[d] * dec_ref[c][0:1, :] + _dot(s_bf, mc_ref[c]) + dl_ref[c])
        return tuple(new)

    zero = jnp.zeros((LANES, LANES), F32)
    lax.fori_loop(0, n_chunks, phase2, (zero, zero))


def _wkv(r, v, kk, kd0, kd1, ag0, ag1, lw0, lw1, b):
    spec = pl.BlockSpec((None, SEQ, LANES), lambda bi, hp: (bi, 0, hp))
    args = [a.reshape(b, SEQ, C) for a in (r, v, kk, kd0, kd1, ag0, ag1, lw0, lw1)]
    n_chunks = SEQ // WKV_CHUNK
    kern = functools.partial(_wkv_kernel, chunk=WKV_CHUNK, n_chunks=n_chunks)
    per_dir = [pltpu.VMEM((SEQ, LANES), F32),
               pltpu.VMEM((n_chunks, LANES, LANES), BF16),
               pltpu.VMEM((n_chunks, LANES, LANES), BF16),
               pltpu.VMEM((n_chunks, LANES, LANES), F32),
               pltpu.VMEM((n_chunks, 8, LANES), F32)]
    scratch = [s for pair in zip(per_dir, per_dir) for s in pair]
    return pl.pallas_call(
        kern,
        grid=(b, C // LANES),
        in_specs=[spec] * 9,
        out_specs=spec,
        out_shape=jax.ShapeDtypeStruct((b, SEQ, C), F32),
        scratch_shapes=scratch,
        compiler_params=_params("parallel", "parallel"),
        name="wkv7",
    )(*args)


def _rwkv_post_kernel(y_ref, r_ref, kd0_ref, kd1_ref, v_ref, g_ref, x_ref, lng_ref, lnb_ref,
                      rk_ref, wo_ref, o_ref):
    ones = _head_ones(COL_CHUNK)
    y = y_ref[...]
    dev = y - _head_sum(y, ones) * (1.0 / HD)
    var = _head_sum(dev * dev, ones) * (1.0 / HD)
    yn = dev * lax.rsqrt(var + RW_GN_EPS) * lng_ref[...] + lnb_ref[...]
    k_bonus = 0.5 * (kd0_ref[...].astype(F32) + kd1_ref[...].astype(F32))
    bonus = _head_sum(r_ref[...].astype(F32) * k_bonus * rk_ref[...], ones) * v_ref[...].astype(F32)
    out = ((yn + bonus) * g_ref[...].astype(F32)).astype(BF16)
    o_ref[...] = x_ref[...] + _dot(out, wo_ref[...])


def _rwkv_post(y, r, kd0, kd1, v, g, x, ln_g, ln_b, r_k, w_o, tm=256):
    t = x.shape[0]
    row = pl.BlockSpec((tm, C), lambda i: (i, 0))
    vec = _const_spec((1, C))
    return pl.pallas_call(
        _rwkv_post_kernel,
        grid=(t // tm,),
        in_specs=[row] * 7 + [vec, vec, vec, _const_spec((C, C))],
        out_specs=row,
        out_shape=jax.ShapeDtypeStruct((t, C), F32),
        compiler_params=_params("parallel"),
        name="rwkv_post",
    )(y, r, kd0, kd1, v, g, x, ln_g.reshape(1, C), ln_b.reshape(1, C), r_k.reshape(1, C),
      w_o.astype(BF16))


def _lambda_init(layer_idx):
    return 0.8 - 0.6 * math.exp(-0.3 * layer_idx)


def _head_gain(q_gain, n_q, k_gain, n_k, n_v):
    scale = HD ** -0.5
    return jnp.concatenate([jnp.tile(q_gain * scale, n_q // HD), jnp.tile(k_gain, n_k // HD),
                            jnp.ones((n_v,), F32)])


def _encode(x, b, p):
    t = b * SEQ
    x = x.reshape(t, C)
    pos = jnp.arange(SEQ, dtype=F32)

    hn = _head_gain(p['a_q_norm'], C, p['a_k_norm'], C, C)
    qkv = _proj(x, p['n0_attn'], p['a_w_qkv'].astype(BF16), hn, 2 * C,
                rope=_rope_tables(pos, None, HD // 2), rope_half=HD // 2)
    lam_init = _lambda_init(0)
    lam = (jnp.exp(jnp.sum(p['a_lq1'] * p['a_lk1'])) - jnp.exp(jnp.sum(p['a_lq2'] * p['a_lk2']))
           + lam_init).reshape(1)
    o = _diff_attn(qkv.reshape(b, SEQ, 3 * C), lam,
                   (p['a_subln'] * (1.0 - lam_init)).reshape(1, LANES), b)
    x = _out_proj(o.reshape(t, C), p['a_w_o'].astype(BF16), x)
    x = _ffn(x, p['n0_ffn'], p['f0_w_in'], p['f0_conv_w'], p['f0_conv_b'], p['f0_w_out'])

    kvw = GQ_KV * HD
    w = p['b_w_qkv']

    def dup(wc):
        wc = wc.reshape(C, GQ_KV, 1, HD)
        return jnp.broadcast_to(wc, (C, GQ_KV, 2, HD)).reshape(C, 2 * kvw)

    w_dup = jnp.concatenate([w[:, :C], dup(w[:, C:C + kvw]), dup(w[:, C + kvw:])], axis=1)
    hn = _head_gain(p['b_q_norm'], C, p['b_k_norm'], 2 * kvw, 2 * kvw)
    tok = jnp.arange(SEQ)
    rope = _rope_tables((tok // GRID_W).astype(F32), (tok % GRID_W).astype(F32), HD // 4)
    qkv = _proj(x, p['n1_attn'], w_dup.astype(BF16), hn, C + 2 * kvw, rope=rope,
                rope_half=HD // 4)
    o = _gqa(qkv.reshape(b, SEQ, C + 4 * kvw), b)
    x = _out_proj(o.reshape(t, C), p['b_w_o'].astype(BF16), x)
    x = _ffn(x, p['n1_ffn'], p['f1_w_in'], p['f1_conv_w'], p['f1_conv_b'], p['f1_w_out'])

    hn = _head_gain(p['c_q_norm'], C, p['c_k_norm'], C, C)
    qkv = _proj(x, p['n2_attn'], p['c_w_qkv'].astype(BF16), hn, 2 * C)
    o = _na_attn(qkv.reshape(b, SEQ, 3 * C), _na_bias_table(p['c_rel_bias']), b)
    x = _out_proj(o.reshape(t, C), p['c_w_o'].astype(BF16), x)
    x = _ffn(x, p['n2_ffn'], p['f2_w_in'], p['f2_conv_w'], p['f2_conv_b'], p['f2_w_out'])

    r, v, g, kk, kd0, kd1, ag0, ag1, lw0, lw1 = _rwkv_pre(
        x, p['n3_attn'], p['d_mu'], p['d_w_r'], p['d_w_k'], p['d_w_v'], p['d_g1'], p['d_g2'],
        p['d_k_k'], p['d_k_a'], p['d_w0'], p['d_w1'], p['d_w2'], p['d_a0'], p['d_a1'], p['d_a2'])
    y = _wkv(r, v, kk, kd0, kd1, ag0, ag1, lw0, lw1, b)
    x = _rwkv_post(y.reshape(t, C), r, kd0, kd1, v, g, x, p['d_ln_g'], p['d_ln_b'], p['d_r_k'],
                   p['d_w_o'])
    x = _ffn(x, p['n3_ffn'], p['f3_w_in'], p['f3_conv_w'], p['f3_conv_b'], p['f3_w_out'])
    return x.reshape(b, SEQ, C)


def kernel(x_prompt, x_sample,
           n0_attn, n0_ffn, n1_attn, n1_ffn, n2_attn, n2_ffn, n3_attn, n3_ffn, a_w_qkv,
           a_q_norm, a_k_norm, a_lq1, a_lk1, a_lq2, a_lk2, a_subln, a_w_o, b_w_qkv,
           b_q_norm, b_k_norm, b_w_o, c_w_qkv, c_q_norm, c_k_norm, c_rel_bias, c_w_o, d_mu,
           d_w_r, d_w_k, d_w_v, d_w_o, d_g1, d_g2, d_k_k, d_k_a, d_r_k, d_ln_g, d_ln_b,
           d_w0, d_w1, d_w2, d_a0, d_a1, d_a2, f0_w_in, f0_conv_w, f0_conv_b, f0_w_out,
           f1_w_in, f1_conv_w, f1_conv_b, f1_w_out, f2_w_in, f2_conv_w, f2_conv_b, f2_w_out,
           f3_w_in, f3_conv_w, f3_conv_b, f3_w_out):
    p = dict(locals())
    xs = (p.pop('x_prompt'), p.pop('x_sample'))
    return tuple(_encode(x, x.shape[0], p) for x in xs)
```

```python
import functools
import math

import numpy as np
import jax
import jax.numpy as jnp
from jax import lax
from jax.experimental import pallas as pl
from jax.experimental.pallas import tpu as pltpu

F32 = jnp.float32
BF16 = jnp.bfloat16

C = 1024
HD = 64
LANES = 128
SEQ = 2048
GRID_W = 64
ROPE_THETA = 10000.0
NORM_EPS = 1e-6
DA_HEADS = 8
DA_SUBLN_EPS = 1e-5
GQ_KV = 4
NA_WIN_ROWS = 8
NA_WIN_COLS = 16
NA_QROWS = 4
NA_KROWS = 12
RW_GN_EPS = 64e-5
FFN_HIDDEN = 2816
WKV_CHUNK = 64
LOG2E = math.log2(math.e)

COL_CHUNK = 256
VMEM_LIMIT = 56 * 1024 * 1024


def _params(*sem):
    return pltpu.CompilerParams(dimension_semantics=sem, vmem_limit_bytes=VMEM_LIMIT)


def _rms(x, g, eps):
    return x * lax.rsqrt(jnp.mean(x * x, axis=-1, keepdims=True) + eps) * g


def _split(x):
    hi = x.astype(BF16)
    lo = (x - hi.astype(F32)).astype(BF16)
    return hi, lo


def _head_ones(n):
    r = lax.broadcasted_iota(jnp.int32, (n, n), 0)
    c = lax.broadcasted_iota(jnp.int32, (n, n), 1)
    return jnp.where((r // HD) == (c // HD), 1.0, 0.0).astype(BF16)


def _head_sum(x, ones):
    w = ones.shape[0]
    out = []
    for c in range(x.shape[1] // w):
        hi, lo = _split(x[:, c * w:(c + 1) * w])
        out.append(_dot(hi, ones) + _dot(lo, ones))
    return out[0] if len(out) == 1 else jnp.concatenate(out, axis=1)


def _dot(a, b):
    return jnp.dot(a, b, preferred_element_type=F32)


def _dot_nt(a, b):
    return lax.dot_general(a, b, (((1,), (1,)), ((), ())), preferred_element_type=F32)


def _shift_rows(u, prev_row, next_row):
    n = u.shape[0]
    up = pltpu.roll(u, 1, 0)
    dn = pltpu.roll(u, n - 1, 0)
    row = lax.broadcasted_iota(jnp.int32, (8, u.shape[1]), 0)
    up = jnp.concatenate([jnp.where(row == 0, prev_row, up[:8]), up[8:]], axis=0)
    dn = jnp.concatenate([dn[:n - 8], jnp.where(row == 7, next_row, dn[n - 8:])], axis=0)
    return up, dn


def _lo_mask(shape):
    return lax.broadcasted_iota(jnp.int32, shape, len(shape) - 1) < HD


def _stack_heads(x):
    m = _lo_mask(x.shape)
    zero = jnp.zeros_like(x)
    return jnp.concatenate([jnp.where(m, x, zero), jnp.where(m, zero, x)], axis=-2)


def _proj_kernel(*refs, n_chunks, n_norm_chunks, rope_half):
    if rope_half:
        x_ref, g_ref, w_ref, hn_ref, cos_ref, sin_ref, o_ref = refs
    else:
        x_ref, g_ref, w_ref, hn_ref, o_ref = refs
    xn = _rms(x_ref[...], g_ref[...], NORM_EPS).astype(BF16)
    ones = _head_ones(COL_CHUNK)
    if rope_half:
        lane = lax.broadcasted_iota(jnp.int32, (1, COL_CHUNK), 1)
        first = (lane % (2 * rope_half)) < rope_half
    def mm(c):
        return _dot(xn, w_ref[:, c * COL_CHUNK:(c + 1) * COL_CHUNK])

    y_next = mm(0)
    for c in range(n_chunks):
        cols = slice(c * COL_CHUNK, (c + 1) * COL_CHUNK)
        y = y_next
        if c + 1 < n_chunks:
            y_next = mm(c + 1)
        if c < n_norm_chunks:
            ms = _dot((y * y).astype(BF16), ones) * (1.0 / HD)
            y = y * lax.rsqrt(ms + NORM_EPS) * hn_ref[:, cols]
            if rope_half:
                rot = jnp.where(first, pltpu.roll(y, COL_CHUNK - rope_half, 1),
                                pltpu.roll(y, rope_half, 1))
                y = y * cos_ref[...] + rot * sin_ref[...]
        o_ref[:, cols] = y.astype(BF16)


def _proj(x, g, w, hn, n_norm_cols, rope=None, rope_half=0, tm=512):
    t = x.shape[0]
    n = w.shape[1]
    tiles_per_seq = SEQ // tm
    in_specs = [
        pl.BlockSpec((tm, C), lambda i: (i, 0)),
        pl.BlockSpec((1, C), lambda i: (0, 0)),
        pl.BlockSpec((C, n), lambda i: (0, 0)),
        pl.BlockSpec((1, n), lambda i: (0, 0)),
    ]
    args = [x, g.reshape(1, C), w, hn.reshape(1, n)]
    if rope_half:
        spec = pl.BlockSpec((tm, COL_CHUNK), lambda i: (i % tiles_per_seq, 0))
        in_specs += [spec, spec]
        args += [rope[0], rope[1]]
    kern = functools.partial(_proj_kernel, n_chunks=n // COL_CHUNK,
                             n_norm_chunks=n_norm_cols // COL_CHUNK, rope_half=rope_half)
    return pl.pallas_call(
        kern,
        grid=(t // tm,),
        in_specs=in_specs,
        out_specs=pl.BlockSpec((tm, n), lambda i: (i, 0)),
        out_shape=jax.ShapeDtypeStruct((t, n), BF16),
        compiler_params=_params("parallel"),
        name="proj",
    )(*args)


def _rope_tables(pos_first, pos_second, half):
    inv = ROPE_THETA ** (-jnp.arange(half, dtype=F32) / half)

    def one(pos):
        ang = pos[:, None] * inv[None, :]
        cos = jnp.concatenate([jnp.cos(ang), jnp.cos(ang)], axis=-1)
        sin = jnp.concatenate([-jnp.sin(ang), jnp.sin(ang)], axis=-1)
        return cos, sin

    c1, s1 = one(pos_first)
    if pos_second is None:
        cos, sin = c1, s1
    else:
        c2, s2 = one(pos_second)
        cos = jnp.concatenate([c1, c2], axis=-1)
        sin = jnp.concatenate([s1, s2], axis=-1)
    reps = COL_CHUNK // cos.shape[-1]
    return jnp.tile(cos, (1, reps)), jnp.tile(sin, (1, reps))


def _out_kernel(a_ref, w_ref, x_ref, o_ref):
    o_ref[...] = x_ref[...] + _dot(a_ref[...], w_ref[...])


def _out_proj(a, w, x, tm=512):
    t = x.shape[0]
    return pl.pallas_call(
        _out_kernel,
        grid=(t // tm,),
        in_specs=[pl.BlockSpec((tm, C), lambda i: (i, 0)),
                  pl.BlockSpec((C, C), lambda i: (0, 0)),
                  pl.BlockSpec((tm, C), lambda i: (i, 0))],
        out_specs=pl.BlockSpec((tm, C), lambda i: (i, 0)),
        out_shape=jax.ShapeDtypeStruct((t, C), F32),
        compiler_params=_params("parallel"),
        name="out_proj",
    )(a, w, x)


HALO = 8


def _halo_rows(xp_ref, xn_ref, tiles_per_seq):
    i = pl.program_id(0)
    pos = i % tiles_per_seq
    halo = jnp.concatenate([xp_ref[...], xn_ref[...]], axis=0)
    row = lax.broadcasted_iota(jnp.int32, halo.shape, 0)
    has_prev = (pos != 0).astype(jnp.int32)
    has_next = (pos != tiles_per_seq - 1).astype(jnp.int32)
    keep = jnp.where(row < HALO, has_prev, has_next)
    return jnp.where(keep != 0, halo, 0.0)


def _ffn_kernel(x_ref, xp_ref, xn_ref, g_ref, wi_ref, cw_ref, cb_ref, wo_ref, o_ref, hid_ref, *,
                tm, tiles_per_seq):
    x = x_ref[...]
    xe = jnp.concatenate([x, _halo_rows(xp_ref, xn_ref, tiles_per_seq)], axis=0)
    xne = _rms(xe, g_ref[...], NORM_EPS).astype(BF16)

    def in_proj(c):
        gate_cols = slice(c * COL_CHUNK, (c + 1) * COL_CHUNK)
        val_cols = slice(FFN_HIDDEN + c * COL_CHUNK, FFN_HIDDEN + (c + 1) * COL_CHUNK)
        return _dot(xne, wi_ref[:, gate_cols]), _dot(xne, wi_ref[:, val_cols])

    def conv(u, cols):
        um = u[:tm]
        up, dn = _shift_rows(um, u[tm + HALO - 1:tm + HALO], u[tm + HALO:tm + HALO + 1])
        cw = cw_ref[:, cols]
        return up * cw[0:1] + um * cw[1:2] + dn * cw[2:3] + cb_ref[:, cols]

    n_chunks = FFN_HIDDEN // COL_CHUNK
    u_next = in_proj(0)
    for c in range(n_chunks):
        ug, uv = u_next
        if c + 1 < n_chunks:
            u_next = in_proj(c + 1)
        cols = slice(c * COL_CHUNK, (c + 1) * COL_CHUNK)
        gate = conv(ug, cols)
        val = conv(uv, slice(FFN_HIDDEN + c * COL_CHUNK, FFN_HIDDEN + (c + 1) * COL_CHUNK))
        hid_ref[:, cols] = (gate * jax.nn.sigmoid(gate) * val).astype(BF16)
    o_ref[...] = x + _dot(hid_ref[...], wo_ref[...])


def _halo_specs(tm, t):
    per = tm // HALO
    last = t // HALO - 1
    return [pl.BlockSpec((HALO, C), lambda i: (jnp.maximum(i * per - 1, 0), 0)),
            pl.BlockSpec((HALO, C), lambda i: (jnp.minimum((i + 1) * per, last), 0))]


def _const_spec(shape):
    return pl.BlockSpec(shape, lambda i: (0,) * len(shape))


def _ffn(x, g, w_in, conv_w, conv_b, w_out, tm=512):
    t = x.shape[0]
    kern = functools.partial(_ffn_kernel, tm=tm, tiles_per_seq=SEQ // tm)
    return pl.pallas_call(
        kern,
        grid=(t // tm,),
        in_specs=[pl.BlockSpec((tm, C), lambda i: (i, 0))] + _halo_specs(tm, t) + [
            _const_spec((1, C)),
            _const_spec((C, 2 * FFN_HIDDEN)),
            _const_spec((3, 2 * FFN_HIDDEN)),
            _const_spec((1, 2 * FFN_HIDDEN)),
            _const_spec((FFN_HIDDEN, C))],
        out_specs=pl.BlockSpec((tm, C), lambda i: (i, 0)),
        out_shape=jax.ShapeDtypeStruct((t, C), F32),
        scratch_shapes=[pltpu.VMEM((tm, FFN_HIDDEN), BF16)],
        compiler_params=_params("parallel"),
        name="conv_ffn",
    )(x, x, x, g.reshape(1, C), w_in.astype(BF16), conv_w, conv_b.reshape(1, -1),
      w_out.astype(BF16))


ATTN_SUB = 256


def _softmax_rows(s):
    p = jnp.exp2(s - jnp.max(s, axis=-1, keepdims=True))
    return p, 1.0 / jnp.sum(p, axis=-1, keepdims=True)


def _diff_attn_kernel(lam_ref, q_ref, k_ref, v_ref, sg_ref, o_ref, *, tq):
    k = k_ref[...]
    v = v_ref[...]
    n = ATTN_SUB // 2
    n_tiles = tq // n

    def scores(t):
        return _dot_nt(_stack_heads(q_ref[t * n:(t + 1) * n, :]), k)

    s_next = scores(0)
    for t in range(n_tiles):
        s = s_next
        if t + 1 < n_tiles:
            s_next = scores(t + 1)
        p, inv = _softmax_rows(s)
        pd = p[:n] * inv[:n] - p[n:] * (lam_ref[0] * inv[n:])
        o = _dot(pd.astype(BF16), v)
        o_ref[t * n:(t + 1) * n, :] = _rms(o, sg_ref[...], DA_SUBLN_EPS).astype(BF16)


def _diff_attn(qkv, lam, subln_gain, b, tq=512):
    q_blocks = C // LANES
    kern = functools.partial(_diff_attn_kernel, tq=tq)
    return pl.pallas_call(
        kern,
        grid=(b, DA_HEADS, SEQ // tq),
        in_specs=[pl.BlockSpec(memory_space=pltpu.SMEM),
                  pl.BlockSpec((None, tq, LANES), lambda bi, h, qi: (bi, qi, h)),
                  pl.BlockSpec((None, SEQ, LANES), lambda bi, h, qi: (bi, 0, q_blocks + h)),
                  pl.BlockSpec((None, SEQ, LANES), lambda bi, h, qi: (bi, 0, 2 * q_blocks + h)),
                  pl.BlockSpec((1, LANES), lambda bi, h, qi: (0, 0))],
        out_specs=pl.BlockSpec((None, tq, LANES), lambda bi, h, qi: (bi, qi, h)),
        out_shape=jax.ShapeDtypeStruct((b, SEQ, C), BF16),
        compiler_params=_params("parallel", "parallel", "parallel"),
        name="diff_attn",
    )(lam, qkv, qkv, qkv, subln_gain)


def _gqa_kernel(q_ref, k_ref, v_ref, o_ref, *, tq):
    k = k_ref[...]
    v = v_ref[...]
    n = ATTN_SUB // 2
    m = _lo_mask((n, LANES))
    tiles = [(slice(t * n, (t + 1) * n), slice(half * LANES, (half + 1) * LANES))
             for t in range(tq // n) for half in range(2)]

    def scores(tile):
        return _dot_nt(_stack_heads(q_ref[tile]), k)

    s_next = scores(tiles[0])
    for i, tile in enumerate(tiles):
        s = s_next
        if i + 1 < len(tiles):
            s_next = scores(tiles[i + 1])
        p, inv = _softmax_rows(s)
        o = _dot(p.astype(BF16), v) * inv
        o_ref[tile] = jnp.where(m, o[:n], o[n:]).astype(BF16)


def _gqa(qkv, b, tq=512):
    kern = functools.partial(_gqa_kernel, tq=tq)
    k0 = C // LANES
    return pl.pallas_call(
        kern,
        grid=(b, GQ_KV, SEQ // tq),
        in_specs=[pl.BlockSpec((None, tq, 2 * LANES), lambda bi, j, qi: (bi, qi, j)),
                  pl.BlockSpec((None, SEQ, LANES), lambda bi, j, qi: (bi, 0, k0 + j)),
                  pl.BlockSpec((None, SEQ, LANES), lambda bi, j, qi: (bi, 0, k0 + GQ_KV + j))],
        out_specs=pl.BlockSpec((None, tq, 2 * LANES), lambda bi, j, qi: (bi, qi, j)),
        out_shape=jax.ShapeDtypeStruct((b, SEQ, C), BF16),
        compiler_params=_params("parallel", "parallel", "parallel"),
        name="gqa_attn",
    )(qkv, qkv, qkv)


NA_TQ = NA_QROWS * GRID_W
NA_TK = NA_KROWS * GRID_W
NA_KBLK = NA_TK // NA_TQ
NA_STEPS = SEQ // NA_TQ
NA_BATCH = 8


def _na_kernel(q_ref, k0_ref, k1_ref, k2_ref, v0_ref, v1_ref, v2_ref, bias_ref, o_ref, *, nb):
    n = ATTN_SUB // 2
    m = _lo_mask((n, LANES))
    tiles = [(bi, t) for bi in range(nb) for t in range(NA_TQ // n)]

    def scores(tile):
        bi, t = tile
        k = jnp.concatenate([k0_ref[bi], k1_ref[bi], k2_ref[bi]], axis=0)
        return _dot_nt(_stack_heads(q_ref[bi, t * n:(t + 1) * n, :]), k)

    s_next = scores(tiles[0])
    for i, (bi, t) in enumerate(tiles):
        s = s_next
        if i + 1 < len(tiles):
            s_next = scores(tiles[i + 1])
        rows = slice(t * n, (t + 1) * n)
        bias = jnp.concatenate([bias_ref[rows, :],
                                bias_ref[NA_TQ + t * n:NA_TQ + (t + 1) * n, :]], axis=0)
        p, inv = _softmax_rows(s + bias)
        v = jnp.concatenate([v0_ref[bi], v1_ref[bi], v2_ref[bi]], axis=0)
        o = _dot(p.astype(BF16), v) * inv
        o_ref[bi, rows, :] = jnp.where(m, o[:n], o[n:]).astype(BF16)


def _na_key_block(i):
    return jnp.clip(i - 1, 0, NA_STEPS - NA_KBLK)


def _na_pattern(i):
    return jnp.minimum(i, 2) + i // (NA_STEPS - 1)


def _na_attn(qkv, bias, b):
    nblk = C // LANES
    nb = math.gcd(b, NA_BATCH)

    def kv_spec(base, j):
        return pl.BlockSpec((nb, NA_TQ, LANES),
                            lambda i, hp, bi: (bi, _na_key_block(i) + j, base + hp))

    return pl.pallas_call(
        functools.partial(_na_kernel, nb=nb),
        grid=(NA_STEPS, nblk, b // nb),
        in_specs=[pl.BlockSpec((nb, NA_TQ, LANES), lambda i, hp, bi: (bi, i, hp))]
        + [kv_spec(nblk, j) for j in range(NA_KBLK)]
        + [kv_spec(2 * nblk, j) for j in range(NA_KBLK)]
        + [pl.BlockSpec((None, None, 2 * NA_TQ, NA_TK),
                        lambda i, hp, bi: (_na_pattern(i), hp, 0, 0))],
        out_specs=pl.BlockSpec((nb, NA_TQ, LANES), lambda i, hp, bi: (bi, i, hp)),
        out_shape=jax.ShapeDtypeStruct((b, SEQ, C), BF16),
        compiler_params=_params("parallel", "parallel", "parallel"),
        name="na_attn",
    )(qkv, qkv, qkv, qkv, qkv, qkv, qkv, bias)


def _na_bias_table(rel_bias):
    rows = SEQ // GRID_W
    wr, wc = NA_WIN_ROWS, NA_WIN_COLS
    heads = rel_bias.shape[0]
    cols = np.arange(GRID_W)
    col_start = np.clip(cols - wc // 2, 0, GRID_W - wc)
    col_mask = (cols[None, :] >= col_start[:, None]) & (cols[None, :] < col_start[:, None] + wc)
    dc = np.clip(cols[None, :] - cols[:, None] + (wc - 1), 0, 2 * wc - 2)
    onehot = (dc.reshape(-1)[None, :] == np.arange(2 * wc - 1)[:, None]).astype(np.float32)
    tiles = jnp.dot(rel_bias.reshape(heads * (2 * wr - 1), 2 * wc - 1), jnp.asarray(onehot),
                    precision=lax.Precision.HIGHEST)
    tiles = tiles.reshape(heads, 2 * wr - 1, GRID_W, GRID_W)
    tiles = jnp.where(jnp.asarray(col_mask)[None, None], tiles * LOG2E, -1e30)
    tiles = jnp.concatenate([tiles, jnp.full((heads, 1, GRID_W, GRID_W), -1e30, F32)], axis=1)
    n_dr = 2 * wr - 1
    idx = np.full((4, NA_QROWS, NA_KROWS), n_dr, np.int32)
    starts = [(0, 0), (NA_QROWS, 0), (2 * NA_QROWS, NA_QROWS), (rows - NA_QROWS, rows - NA_KROWS)]
    for p, (r0, ks) in enumerate(starts):
        for a in range(NA_QROWS):
            r = r0 + a
            rs = min(max(r - wr // 2, 0), rows - wr)
            for bb in range(NA_KROWS):
                j = ks + bb
                if rs <= j < rs + wr:
                    idx[p, a, bb] = j - r + (wr - 1)
    big = tiles[:, jnp.asarray(idx)]
    big = big.transpose(1, 0, 2, 4, 3, 5)
    return big.reshape(4, heads // 2, 2 * NA_TQ, NA_TK)


def _rwkv_pre_kernel(x_ref, xp_ref, xn_ref, g_ref, mu_ref, wr_ref, wk_ref, wv_ref, g1_ref, g2_ref,
                     w1_ref, w2_ref, a1_ref, a2_ref, w0_ref, a0_ref, kk_ref, ka_ref,
                     r_o, v_o, g_o, kk_o, kd0_o, kd1_o, ag0_o, ag1_o, lw0_o, lw1_o, *,
                     tiles_per_seq):
    g = g_ref[...]
    h = _rms(x_ref[...], g, NORM_EPS)
    hh = _rms(_halo_rows(xp_ref, xn_ref, tiles_per_seq), g, NORM_EPS)
    up, dn = _shift_rows(h, hh[HALO - 1:HALO], hh[HALO:HALO + 1])
    xx = 0.5 * (up + dn) - h

    def mix(j):
        return (h + xx * mu_ref[j:j + 1, :]).astype(BF16)

    tw = jnp.tanh(_dot(mix(1), w1_ref[...]))
    ta = _dot(mix(4), a1_ref[...])
    gate = jax.nn.sigmoid(_dot(mix(5), g1_ref[...]))
    lo = _lo_mask(tw.shape)
    sels = (lo, jnp.logical_not(lo))
    zs = [_dot(jnp.where(s, tw, 0.0).astype(BF16), w2_ref[...]) for s in sels]
    az = [_dot(jnp.where(s, ta, 0.0).astype(BF16), a2_ref[...]) for s in sels]
    k = _dot(mix(2), wk_ref[...])
    r_o[...] = _dot(mix(0), wr_ref[...]).astype(BF16)
    v_o[...] = _dot(mix(3), wv_ref[...]).astype(BF16)
    g_o[...] = _dot(gate.astype(BF16), g2_ref[...]).astype(BF16)

    kkv = k * kk_ref[...]
    nrm = jnp.sqrt(_head_sum(kkv * kkv, _head_ones(COL_CHUNK)))
    kk_o[...] = (kkv / jnp.maximum(nrm, 1e-12)).astype(BF16)
    ka = ka_ref[...]
    for d, (kd_o, ag_o, lw_o) in enumerate(((kd0_o, ag0_o, lw0_o), (kd1_o, ag1_o, lw1_o))):
        lw_o[...] = -math.exp(-0.5) * jax.nn.sigmoid(w0_ref[d:d + 1, :] + zs[d])
        a = jax.nn.sigmoid(a0_ref[d:d + 1, :] + az[d])
        ag_o[...] = a.astype(BF16)
        kd_o[...] = (k * (1.0 + (a - 1.0) * ka)).astype(BF16)


def _rwkv_pre(x, g, mu, w_r, w_k, w_v, g1, g2, k_k, k_a, w0, w1, w2, a0, a1, a2, tm=256):
    t = x.shape[0]
    lora = w1.shape[-1]
    w1c = jnp.concatenate([w1[0], w1[1]], axis=1).astype(BF16)
    a1c = jnp.concatenate([a1[0], a1[1]], axis=1).astype(BF16)
    w2c = w2.reshape(2 * lora, C).astype(BF16)
    a2c = a2.reshape(2 * lora, C).astype(BF16)
    assert 2 * lora == LANES
    glora = g1.shape[-1]
    row = pl.BlockSpec((tm, C), lambda i: (i, 0))
    kern = functools.partial(_rwkv_pre_kernel, tiles_per_seq=SEQ // tm)
    bf = jax.ShapeDtypeStruct((t, C), BF16)
    f32 = jax.ShapeDtypeStruct((t, C), F32)
    return pl.pallas_call(
        kern,
        grid=(t // tm,),
        in_specs=[row] + _halo_specs(tm, t) + [
            _const_spec((1, C)), _const_spec((6, C)),
            _const_spec((C, C)), _const_spec((C, C)), _const_spec((C, C)),
            _const_spec((C, glora)), _const_spec((glora, C)),
            _const_spec((C, LANES)), _const_spec((LANES, C)),
            _const_spec((C, LANES)), _const_spec((LANES, C)),
            _const_spec((2, C)), _const_spec((2, C)), _const_spec((1, C)), _const_spec((1, C))],
        out_specs=[row] * 10,
        out_shape=[bf] * 8 + [f32] * 2,
        compiler_params=_params("parallel"),
        name="rwkv_pre",
    )(x, x, x, g.reshape(1, C), mu, w_r.astype(BF16), w_k.astype(BF16), w_v.astype(BF16),
      g1.astype(BF16), g2.astype(BF16), w1c, w2c, a1c, a2c, w0, a0,
      k_k.reshape(1, C), k_a.reshape(1, C))


WKV_GROUP = 16


def _wkv_kernel(r_ref, v_ref, kk_ref, kd0_ref, kd1_ref, ag0_ref, ag1_ref, lw0_ref, lw1_ref,
                y_ref, *scratch, chunk, n_chunks):
    n2 = 2 * chunk
    ri = lax.broadcasted_iota(jnp.int32, (n2, 2 * n2), 0)
    ci = lax.broadcasted_iota(jnp.int32, (n2, 2 * n2), 1)
    same_head = (ri // chunk) == ((ci % n2) // chunk)
    tr, tc = ri % chunk, ci % chunk
    qi = lax.broadcasted_iota(jnp.int32, (n2, n2), 0)
    qj = lax.broadcasted_iota(jnp.int32, (n2, n2), 1)
    eye = jnp.where(qi == qj, 1.0, 0.0)
    level_masks = [(qi // 2) == (qj // 2)]
    size = 2
    while size < chunk:
        level_masks.append(((qi // (2 * size)) == (qj // (2 * size)))
                           & ((qi // size) != (qj // size)))
        size *= 2
    n_scr = len(scratch) // 2
    dirs = ((kd0_ref, ag0_ref, lw0_ref) + tuple(scratch[:n_scr]),
            (kd1_ref, ag1_ref, lw1_ref) + tuple(scratch[n_scr:]))

    def chunk_prefix(x):
        pos = lax.broadcasted_iota(jnp.int32, x.shape, 0) % chunk
        shift = 1
        while shift < chunk:
            x = x + jnp.where(pos >= shift, pltpu.roll(x, shift, 0), 0.0)
            shift *= 2
        return x

    masks = ((same_head & (tr > tc), same_head & (tr >= tc)),
             (same_head & (tr < tc), same_head & (tr <= tc)))

    grp = WKV_GROUP
    grows = grp * chunk

    def bmm(a, b):
        return jnp.einsum('gmk,gkn->gmn', a, b, preferred_element_type=F32)


    def phase1(g, carry):
        rows = pl.ds(pl.multiple_of(g * grows, grows), grows)
        slab = pl.ds(pl.multiple_of(g * grp, grp), grp)
        pairs = pl.ds(pl.multiple_of(g * (grp // 2), grp // 2), grp // 2)

        def load(ref):
            return ref[rows, :].reshape(grp, chunk, LANES)

        r = load(r_ref).astype(F32)
        kk = load(kk_ref).astype(F32)
        v_s = _stack_heads(load(v_ref))
        v_pad = jnp.concatenate([jnp.zeros((grp, n2, LANES), BF16), v_s], axis=2)
        y_loc = []
        for d, (kd_ref, ag_ref, lw_ref, rt_ref, mc_ref, dl_ref, dec_ref,
                mc2_ref, dl2_ref, dec2_ref) in enumerate(dirs):
            strict, incl = masks[d]
            lw = lw_ref[rows, :]
            pre = chunk_prefix(lw).reshape(grp, chunk, LANES)
            lw = lw.reshape(grp, chunk, LANES)
            tot = pre[:, chunk - 1:chunk, :]
            if d == 0:
                c_in = pre
                c_ex = pre - lw
            else:
                c_ex = tot - pre
                c_in = c_ex + lw
            kd = load(kd_ref).astype(F32)
            b = kk * load(ag_ref).astype(F32)
            e_neg = jnp.exp(-c_in)
            e_end = jnp.exp(tot - c_in)
            a_t = _stack_heads(-kk * jnp.exp(c_ex))
            r_t = _stack_heads(r * jnp.exp(c_in))
            b_t = (b * e_neg).astype(BF16)
            k_t = (kd * e_neg).astype(BF16)
            lhs = jnp.concatenate([a_t, r_t], axis=1).astype(BF16)
            rhs = jnp.concatenate([b_t, b_t, k_t, k_t], axis=1)
            gram = jnp.einsum('gmd,gnd->gmn', lhs, rhs, preferred_element_type=F32)
            a_abk = jnp.where(strict, gram[:, :n2, :], 0.0)
            a_ab = a_abk[:, :, :n2]
            a_ak = a_abk[:, :, n2:].astype(BF16)
            m_rbk = jnp.where(incl, gram[:, n2:, :], 0.0).astype(BF16)
            inv = eye + jnp.where(level_masks[0], a_ab, 0.0)
            for lvl in range(1, len(level_masks)):
                off = jnp.where(level_masks[lvl], a_ab, 0.0).astype(BF16)
                inv_bf = inv.astype(BF16)
                inv = inv + bmm(bmm(inv_bf, off).astype(BF16), inv_bf)
            x0 = jnp.concatenate([a_t.astype(BF16), bmm(a_ak, v_s).astype(BF16)], axis=2)
            wu = bmm(inv.astype(BF16), x0)
            q = jnp.concatenate([wu.astype(BF16), v_pad], axis=1)
            ry = bmm(m_rbk, q)
            rt_ref[slab] = (r_t + ry[:, :, :LANES]).astype(BF16)
            y_loc.append(ry[:, :chunk, LANES:] + ry[:, chunk:, LANES:])
            bk = jnp.concatenate([_stack_heads(b * e_end), _stack_heads(kd * e_end)],
                                 axis=1).astype(BF16)
            md = jnp.einsum('gkm,gkn->gmn', q, bk, preferred_element_type=F32)
            mc = md[:, :LANES, :]
            dl = md[:, LANES:, :]
            dec = jnp.exp(tot)
            mc_ref[slab] = mc.astype(BF16)
            dl_ref[slab] = dl
            dec_ref[slab] = jnp.broadcast_to(dec, (grp, 8, LANES))
            first, second = (0, 1) if d == 0 else (1, 0)

            def pick(a, which):
                return a.reshape((grp // 2, 2) + a.shape[1:])[:, which]

            dec_a, dec_b = pick(dec, first), pick(dec, second)
            f1 = pick(mc, first) + eye * dec_a
            fd = jnp.concatenate([f1, pick(dl, first)], axis=1).astype(BF16)
            prod = bmm(fd, pick(mc, second).astype(BF16))
            mc2_ref[pairs] = (prod[:, :LANES, :] + pick(mc, first) * dec_b).astype(BF16)
            dl2_ref[pairs] = prod[:, LANES:, :] + pick(dl, first) * dec_b + pick(dl, second)
            dec2_ref[pairs] = jnp.broadcast_to(dec_a * dec_b, (grp // 2, 8, LANES))
        y_ref[rows, :] = (y_loc[0] + y_loc[1]).reshape(grows, LANES)
        return carry

    lax.fori_loop(0, n_chunks // grp, phase1, 0)

    n_pairs = n_chunks // 2

    def add_y(rt_ref, c, s_bf):
        rows = pl.ds(pl.multiple_of(c * chunk, chunk), chunk)
        y = _dot_nt(rt_ref[c], s_bf)
        y_ref[rows, :] = y_ref[rows, :] + (y[:chunk] + y[chunk:])

    def second_chunk(d, pair):
        return 2 * pair + 1 if d == 0 else 2 * pair

    def phase2(step, carry):
        new = []
        for d, (*_ins, rt_ref, mc_ref, dl_ref, dec_ref, mc2_ref, dl2_ref, dec2_ref) in enumerate(dirs):
            pair = step if d == 0 else n_pairs - 1 - step
            prev = jnp.maximum(step - 1, 0) if d == 0 else jnp.minimum(n_pairs - step, n_pairs - 1)
            c1 = 2 * pair if d == 0 else 2 * pair + 1
            s0, mid_prev = carry[d]
            s0_bf = s0.astype(BF16)
            s_next = s0 * dec2_ref[pair][0:1, :] + _dot(s0_bf, mc2_ref[pair]) + dl2_ref[pair]
            mid = s0 * dec_ref[c1][0:1, :] + _dot(s0_bf, mc_ref[c1]) + dl_ref[c1]
            add_y(rt_ref, c1, s0_bf)
            add_y(rt_ref, second_chunk(d, prev), mid_prev)
            new.append((s_next, mid.astype(BF16)))
        return tuple(new)

    zero = jnp.zeros((LANES, LANES), F32)
    init = (zero, zero.astype(BF16))
    final = lax.fori_loop(0, n_pairs, phase2, (init, init))
    for d, (*_ins, rt_ref, _mc, _dl, _dec, _mc2, _dl2, _dec2) in enumerate(dirs):
        last_pair = n_pairs - 1 if d == 0 else 0
        add_y(rt_ref, second_chunk(d, last_pair), final[d][1])


def _wkv(r, v, kk, kd0, kd1, ag0, ag1, lw0, lw1, b):
    spec = pl.BlockSpec((None, SEQ, LANES), lambda bi, hp: (bi, 0, hp))
    args = [a.reshape(b, SEQ, C) for a in (r, v, kk, kd0, kd1, ag0, ag1, lw0, lw1)]
    n_chunks = SEQ // WKV_CHUNK
    kern = functools.partial(_wkv_kernel, chunk=WKV_CHUNK, n_chunks=n_chunks)
    per_dir = [pltpu.VMEM((n_chunks, LANES, LANES), BF16),
               pltpu.VMEM((n_chunks, LANES, LANES), BF16),
               pltpu.VMEM((n_chunks, LANES, LANES), F32),
               pltpu.VMEM((n_chunks, 8, LANES), F32),
               pltpu.VMEM((n_chunks // 2, LANES, LANES), BF16),
               pltpu.VMEM((n_chunks // 2, LANES, LANES), F32),
               pltpu.VMEM((n_chunks // 2, 8, LANES), F32)]
    scratch = per_dir + per_dir
    return pl.pallas_call(
        kern,
        grid=(b, C // LANES),
        in_specs=[spec] * 9,
        out_specs=spec,
        out_shape=jax.ShapeDtypeStruct((b, SEQ, C), F32),
        scratch_shapes=scratch,
        compiler_params=_params("parallel", "parallel"),
        name="wkv7",
    )(*args)


def _rwkv_post_kernel(y_ref, r_ref, kd0_ref, kd1_ref, v_ref, g_ref, x_ref, lng_ref, lnb_ref,
                      rk_ref, wo_ref, o_ref):
    ones = _head_ones(COL_CHUNK)
    y = y_ref[...]
    dev = y - _head_sum(y, ones) * (1.0 / HD)
    var = _head_sum(dev * dev, ones) * (1.0 / HD)
    yn = dev * lax.rsqrt(var + RW_GN_EPS) * lng_ref[...] + lnb_ref[...]
    k_bonus = 0.5 * (kd0_ref[...].astype(F32) + kd1_ref[...].astype(F32))
    bonus = _head_sum(r_ref[...].astype(F32) * k_bonus * rk_ref[...], ones) * v_ref[...].astype(F32)
    out = ((yn + bonus) * g_ref[...].astype(F32)).astype(BF16)
    o_ref[...] = x_ref[...] + _dot(out, wo_ref[...])


def _rwkv_post(y, r, kd0, kd1, v, g, x, ln_g, ln_b, r_k, w_o, tm=256):
    t = x.shape[0]
    row = pl.BlockSpec((tm, C), lambda i: (i, 0))
    vec = _const_spec((1, C))
    return pl.pallas_call(
        _rwkv_post_kernel,
        grid=(t // tm,),
        in_specs=[row] * 7 + [vec, vec, vec, _const_spec((C, C))],
        out_specs=row,
        out_shape=jax.ShapeDtypeStruct((t, C), F32),
        compiler_params=_params("parallel"),
        name="rwkv_post",
    )(y, r, kd0, kd1, v, g, x, ln_g.reshape(1, C), ln_b.reshape(1, C), r_k.reshape(1, C),
      w_o.astype(BF16))


def _lambda_init(layer_idx):
    return 0.8 - 0.6 * math.exp(-0.3 * layer_idx)


def _head_gain(q_gain, n_q, k_gain, n_k, n_v):
    scale = HD ** -0.5 * LOG2E
    return jnp.concatenate([jnp.tile(q_gain * scale, n_q // HD), jnp.tile(k_gain, n_k // HD),
                            jnp.ones((n_v,), F32)])


def _encode(x, b, p):
    t = b * SEQ
    x = x.reshape(t, C)
    pos = jnp.arange(SEQ, dtype=F32)

    hn = _head_gain(p['a_q_norm'], C, p['a_k_norm'], C, C)
    qkv = _proj(x, p['n0_attn'], p['a_w_qkv'].astype(BF16), hn, 2 * C,
                rope=_rope_tables(pos, None, HD // 2), rope_half=HD // 2)
    lam_init = _lambda_init(0)
    lam = (jnp.exp(jnp.sum(p['a_lq1'] * p['a_lk1'])) - jnp.exp(jnp.sum(p['a_lq2'] * p['a_lk2']))
           + lam_init).reshape(1)
    o = _diff_attn(qkv.reshape(b, SEQ, 3 * C), lam,
                   (p['a_subln'] * (1.0 - lam_init)).reshape(1, LANES), b)
    x = _out_proj(o.reshape(t, C), p['a_w_o'].astype(BF16), x)
    x = _ffn(x, p['n0_ffn'], p['f0_w_in'], p['f0_conv_w'], p['f0_conv_b'], p['f0_w_out'])

    kvw = GQ_KV * HD
    w = p['b_w_qkv']

    def dup(wc):
        wc = wc.reshape(C, GQ_KV, 1, HD)
        return jnp.broadcast_to(wc, (C, GQ_KV, 2, HD)).reshape(C, 2 * kvw)

    w_dup = jnp.concatenate([w[:, :C], dup(w[:, C:C + kvw]), dup(w[:, C + kvw:])], axis=1)
    hn = _head_gain(p['b_q_norm'], C, p['b_k_norm'], 2 * kvw, 2 * kvw)
    tok = jnp.arange(SEQ)
    rope = _rope_tables((tok // GRID_W).astype(F32), (tok % GRID_W).astype(F32), HD // 4)
    qkv = _proj(x, p['n1_attn'], w_dup.astype(BF16), hn, C + 2 * kvw, rope=rope,
                rope_half=HD // 4)
    o = _gqa(qkv.reshape(b, SEQ, C + 4 * kvw), b)
    x = _out_proj(o.reshape(t, C), p['b_w_o'].astype(BF16), x)
    x = _ffn(x, p['n1_ffn'], p['f1_w_in'], p['f1_conv_w'], p['f1_conv_b'], p['f1_w_out'])

    hn = _head_gain(p['c_q_norm'], C, p['c_k_norm'], C, C)
    qkv = _proj(x, p['n2_attn'], p['c_w_qkv'].astype(BF16), hn, 2 * C)
    o = _na_attn(qkv.reshape(b, SEQ, 3 * C), _na_bias_table(p['c_rel_bias']), b)
    x = _out_proj(o.reshape(t, C), p['c_w_o'].astype(BF16), x)
    x = _ffn(x, p['n2_ffn'], p['f2_w_in'], p['f2_conv_w'], p['f2_conv_b'], p['f2_w_out'])

    r, v, g, kk, kd0, kd1, ag0, ag1, lw0, lw1 = _rwkv_pre(
        x, p['n3_attn'], p['d_mu'], p['d_w_r'], p['d_w_k'], p['d_w_v'], p['d_g1'], p['d_g2'],
        p['d_k_k'], p['d_k_a'], p['d_w0'], p['d_w1'], p['d_w2'], p['d_a0'], p['d_a1'], p['d_a2'])
    y = _wkv(r, v, kk, kd0, kd1, ag0, ag1, lw0, lw1, b)
    x = _rwkv_post(y.reshape(t, C), r, kd0, kd1, v, g, x, p['d_ln_g'], p['d_ln_b'], p['d_r_k'],
                   p['d_w_o'])
    x = _ffn(x, p['n3_ffn'], p['f3_w_in'], p['f3_conv_w'], p['f3_conv_b'], p['f3_w_out'])
    return x.reshape(b, SEQ, C)


def kernel(x_prompt, x_sample,
           n0_attn, n0_ffn, n1_attn, n1_ffn, n2_attn, n2_ffn, n3_attn, n3_ffn, a_w_qkv,
           a_q_norm, a_k_norm, a_lq1, a_lk1, a_lq2, a_lk2, a_subln, a_w_o, b_w_qkv,
           b_q_norm, b_k_norm, b_w_o, c_w_qkv, c_q_norm, c_k_norm, c_rel_bias, c_w_o, d_mu,
           d_w_r, d_w_k, d_w_v, d_w_o, d_g1, d_g2, d_k_k, d_k_a, d_r_k, d_ln_g, d_ln_b,
           d_w0, d_w1, d_w2, d_a0, d_a1, d_a2, f0_w_in, f0_conv_w, f0_conv_b, f0_w_out,
           f1_w_in, f1_conv_w, f1_conv_b, f1_w_out, f2_w_in, f2_conv_w, f2_conv_b, f2_w_out,
           f3_w_in, f3_conv_w, f3_conv_b, f3_w_out):
    p = dict(locals())
    xs = (p.pop('x_prompt'), p.pop('x_sample'))
    return tuple(_encode(x, x.shape[0], p) for x in xs)
```

```python
import functools
import math

import numpy as np
import jax
import jax.numpy as jnp
from jax import lax
from jax.experimental import pallas as pl
from jax.experimental.pallas import tpu as pltpu

F32 = jnp.float32
BF16 = jnp.bfloat16

C = 1024
HD = 64
LANES = 128
SEQ = 2048
GRID_W = 64
ROPE_THETA = 10000.0
NORM_EPS = 1e-6
DA_HEADS = 8
DA_SUBLN_EPS = 1e-5
GQ_KV = 4
NA_WIN_ROWS = 8
NA_WIN_COLS = 16
NA_QROWS = 4
NA_KROWS = 12
RW_GN_EPS = 64e-5
FFN_HIDDEN = 2816
WKV_CHUNK = 64
LOG2E = math.log2(math.e)

COL_CHUNK = 256
VMEM_LIMIT = 56 * 1024 * 1024


def _params(*sem):
    return pltpu.CompilerParams(dimension_semantics=sem, vmem_limit_bytes=VMEM_LIMIT)


def _rms(x, g, eps):
    return x * lax.rsqrt(jnp.mean(x * x, axis=-1, keepdims=True) + eps) * g


def _split(x):
    hi = x.astype(BF16)
    lo = (x - hi.astype(F32)).astype(BF16)
    return hi, lo


def _head_ones(n):
    r = lax.broadcasted_iota(jnp.int32, (n, n), 0)
    c = lax.broadcasted_iota(jnp.int32, (n, n), 1)
    return jnp.where((r // HD) == (c // HD), 1.0, 0.0).astype(BF16)


def _head_sum(x, ones):
    w = ones.shape[0]
    out = []
    for c in range(x.shape[1] // w):
        hi, lo = _split(x[:, c * w:(c + 1) * w])
        out.append(_dot(hi, ones) + _dot(lo, ones))
    return out[0] if len(out) == 1 else jnp.concatenate(out, axis=1)


def _dot(a, b):
    return jnp.dot(a, b, preferred_element_type=F32)


def _dot_nt(a, b):
    return lax.dot_general(a, b, (((1,), (1,)), ((), ())), preferred_element_type=F32)


def _shift_rows(u, prev_row, next_row):
    n = u.shape[0]
    up = pltpu.roll(u, 1, 0)
    dn = pltpu.roll(u, n - 1, 0)
    row = lax.broadcasted_iota(jnp.int32, (8, u.shape[1]), 0)
    up = jnp.concatenate([jnp.where(row == 0, prev_row, up[:8]), up[8:]], axis=0)
    dn = jnp.concatenate([dn[:n - 8], jnp.where(row == 7, next_row, dn[n - 8:])], axis=0)
    return up, dn


def _lo_mask(shape):
    return lax.broadcasted_iota(jnp.int32, shape, len(shape) - 1) < HD


def _stack_heads(x):
    m = _lo_mask(x.shape)
    zero = jnp.zeros_like(x)
    return jnp.concatenate([jnp.where(m, x, zero), jnp.where(m, zero, x)], axis=-2)


def _proj_kernel(*refs, n_chunks, n_norm_chunks, rope_half):
    if rope_half:
        x_ref, g_ref, w_ref, hn_ref, cos_ref, sin_ref, o_ref = refs
    else:
        x_ref, g_ref, w_ref, hn_ref, o_ref = refs
    xn = _rms(x_ref[...], g_ref[...], NORM_EPS).astype(BF16)
    ones = _head_ones(COL_CHUNK)
    if rope_half:
        lane = lax.broadcasted_iota(jnp.int32, (1, COL_CHUNK), 1)
        first = (lane % (2 * rope_half)) < rope_half
    def mm(c):
        return _dot(xn, w_ref[:, c * COL_CHUNK:(c + 1) * COL_CHUNK])

    y_next = mm(0)
    for c in range(n_chunks):
        cols = slice(c * COL_CHUNK, (c + 1) * COL_CHUNK)
        y = y_next
        if c + 1 < n_chunks:
            y_next = mm(c + 1)
        if c < n_norm_chunks:
            ms = _dot((y * y).astype(BF16), ones) * (1.0 / HD)
            y = y * lax.rsqrt(ms + NORM_EPS) * hn_ref[:, cols]
            if rope_half:
                rot = jnp.where(first, pltpu.roll(y, COL_CHUNK - rope_half, 1),
                                pltpu.roll(y, rope_half, 1))
                y = y * cos_ref[...] + rot * sin_ref[...]
        o_ref[:, cols] = y.astype(BF16)


def _proj(x, g, w, hn, n_norm_cols, rope=None, rope_half=0, tm=512):
    t = x.shape[0]
    n = w.shape[1]
    tiles_per_seq = SEQ // tm
    in_specs = [
        pl.BlockSpec((tm, C), lambda i: (i, 0)),
        pl.BlockSpec((1, C), lambda i: (0, 0)),
        pl.BlockSpec((C, n), lambda i: (0, 0)),
        pl.BlockSpec((1, n), lambda i: (0, 0)),
    ]
    args = [x, g.reshape(1, C), w, hn.reshape(1, n)]
    if rope_half:
        spec = pl.BlockSpec((tm, COL_CHUNK), lambda i: (i % tiles_per_seq, 0))
        in_specs += [spec, spec]
        args += [rope[0], rope[1]]
    kern = functools.partial(_proj_kernel, n_chunks=n // COL_CHUNK,
                             n_norm_chunks=n_norm_cols // COL_CHUNK, rope_half=rope_half)
    return pl.pallas_call(
        kern,
        grid=(t // tm,),
        in_specs=in_specs,
        out_specs=pl.BlockSpec((tm, n), lambda i: (i, 0)),
        out_shape=jax.ShapeDtypeStruct((t, n), BF16),
        compiler_params=_params("parallel"),
        name="proj",
    )(*args)


def _rope_tables(pos_first, pos_second, half):
    inv = ROPE_THETA ** (-jnp.arange(half, dtype=F32) / half)

    def one(pos):
        ang = pos[:, None] * inv[None, :]
        cos = jnp.concatenate([jnp.cos(ang), jnp.cos(ang)], axis=-1)
        sin = jnp.concatenate([-jnp.sin(ang), jnp.sin(ang)], axis=-1)
        return cos, sin

    c1, s1 = one(pos_first)
    if pos_second is None:
        cos, sin = c1, s1
    else:
        c2, s2 = one(pos_second)
        cos = jnp.concatenate([c1, c2], axis=-1)
        sin = jnp.concatenate([s1, s2], axis=-1)
    reps = COL_CHUNK // cos.shape[-1]
    return jnp.tile(cos, (1, reps)), jnp.tile(sin, (1, reps))


def _out_kernel(a_ref, w_ref, x_ref, o_ref):
    o_ref[...] = x_ref[...] + _dot(a_ref[...], w_ref[...])


def _out_proj(a, w, x, tm=512):
    t = x.shape[0]
    return pl.pallas_call(
        _out_kernel,
        grid=(t // tm,),
        in_specs=[pl.BlockSpec((tm, C), lambda i: (i, 0)),
                  pl.BlockSpec((C, C), lambda i: (0, 0)),
                  pl.BlockSpec((tm, C), lambda i: (i, 0))],
        out_specs=pl.BlockSpec((tm, C), lambda i: (i, 0)),
        out_shape=jax.ShapeDtypeStruct((t, C), F32),
        compiler_params=_params("parallel"),
        name="out_proj",
    )(a, w, x)


HALO = 8


def _halo_rows(xp_ref, xn_ref, tiles_per_seq):
    i = pl.program_id(0)
    pos = i % tiles_per_seq
    halo = jnp.concatenate([xp_ref[...], xn_ref[...]], axis=0)
    row = lax.broadcasted_iota(jnp.int32, halo.shape, 0)
    has_prev = (pos != 0).astype(jnp.int32)
    has_next = (pos != tiles_per_seq - 1).astype(jnp.int32)
    keep = jnp.where(row < HALO, has_prev, has_next)
    return jnp.where(keep != 0, halo, 0.0)


def _ffn_kernel(x_ref, xp_ref, xn_ref, g_ref, wi_ref, cw_ref, cb_ref, wo_ref, o_ref, hid_ref, *,
                tm, tiles_per_seq):
    x = x_ref[...]
    xe = jnp.concatenate([x, _halo_rows(xp_ref, xn_ref, tiles_per_seq)], axis=0)
    xne = _rms(xe, g_ref[...], NORM_EPS).astype(BF16)

    def in_proj(c):
        gate_cols = slice(c * COL_CHUNK, (c + 1) * COL_CHUNK)
        val_cols = slice(FFN_HIDDEN + c * COL_CHUNK, FFN_HIDDEN + (c + 1) * COL_CHUNK)
        return _dot(xne, wi_ref[:, gate_cols]), _dot(xne, wi_ref[:, val_cols])

    def conv(u, cols):
        um = u[:tm]
        up, dn = _shift_rows(um, u[tm + HALO - 1:tm + HALO], u[tm + HALO:tm + HALO + 1])
        cw = cw_ref[:, cols]
        return up * cw[0:1] + um * cw[1:2] + dn * cw[2:3] + cb_ref[:, cols]

    n_chunks = FFN_HIDDEN // COL_CHUNK
    u_next = in_proj(0)
    for c in range(n_chunks):
        ug, uv = u_next
        if c + 1 < n_chunks:
            u_next = in_proj(c + 1)
        cols = slice(c * COL_CHUNK, (c + 1) * COL_CHUNK)
        gate = conv(ug, cols)
        val = conv(uv, slice(FFN_HIDDEN + c * COL_CHUNK, FFN_HIDDEN + (c + 1) * COL_CHUNK))
        hid_ref[:, cols] = (gate * jax.nn.sigmoid(gate) * val).astype(BF16)
    o_ref[...] = x + _dot(hid_ref[...], wo_ref[...])


def _halo_specs(tm, t):
    per = tm // HALO
    last = t // HALO - 1
    return [pl.BlockSpec((HALO, C), lambda i: (jnp.maximum(i * per - 1, 0), 0)),
            pl.BlockSpec((HALO, C), lambda i: (jnp.minimum((i + 1) * per, last), 0))]


def _const_spec(shape):
    return pl.BlockSpec(shape, lambda i: (0,) * len(shape), pipeline_mode=pl.Buffered(1))


def _ffn(x, g, w_in, conv_w, conv_b, w_out, tm=512):
    t = x.shape[0]
    kern = functools.partial(_ffn_kernel, tm=tm, tiles_per_seq=SEQ // tm)
    return pl.pallas_call(
        kern,
        grid=(t // tm,),
        in_specs=[pl.BlockSpec((tm, C), lambda i: (i, 0))] + _halo_specs(tm, t) + [
            _const_spec((1, C)),
            _const_spec((C, 2 * FFN_HIDDEN)),
            _const_spec((3, 2 * FFN_HIDDEN)),
            _const_spec((1, 2 * FFN_HIDDEN)),
            _const_spec((FFN_HIDDEN, C))],
        out_specs=pl.BlockSpec((tm, C), lambda i: (i, 0)),
        out_shape=jax.ShapeDtypeStruct((t, C), F32),
        scratch_shapes=[pltpu.VMEM((tm, FFN_HIDDEN), BF16)],
        compiler_params=_params("parallel"),
        name="conv_ffn",
    )(x, x, x, g.reshape(1, C), w_in.astype(BF16), conv_w, conv_b.reshape(1, -1),
      w_out.astype(BF16))


ATTN_SUB = 256


def _softmax_rows(s):
    p = jnp.exp2(s - jnp.max(s, axis=-1, keepdims=True))
    return p, 1.0 / jnp.sum(p, axis=-1, keepdims=True)


def _diff_attn_kernel(lam_ref, q_ref, k_ref, v_ref, sg_ref, o_ref, *, tq):
    k = k_ref[...]
    v = v_ref[...]
    n = ATTN_SUB // 2
    n_tiles = tq // n

    def scores(t):
        return _dot_nt(_stack_heads(q_ref[t * n:(t + 1) * n, :]), k)

    s_next = scores(0)
    for t in range(n_tiles):
        s = s_next
        if t + 1 < n_tiles:
            s_next = scores(t + 1)
        p, inv = _softmax_rows(s)
        o2 = _dot(p.astype(BF16), v) * inv
        o = o2[:n] - lam_ref[0] * o2[n:]
        o_ref[t * n:(t + 1) * n, :] = _rms(o, sg_ref[...], DA_SUBLN_EPS).astype(BF16)


def _diff_attn(qkv, lam, subln_gain, b, tq=512):
    q_blocks = C // LANES
    kern = functools.partial(_diff_attn_kernel, tq=tq)
    return pl.pallas_call(
        kern,
        grid=(b, DA_HEADS, SEQ // tq),
        in_specs=[pl.BlockSpec(memory_space=pltpu.SMEM),
                  pl.BlockSpec((None, tq, LANES), lambda bi, h, qi: (bi, qi, h)),
                  pl.BlockSpec((None, SEQ, LANES), lambda bi, h, qi: (bi, 0, q_blocks + h)),
                  pl.BlockSpec((None, SEQ, LANES), lambda bi, h, qi: (bi, 0, 2 * q_blocks + h)),
                  pl.BlockSpec((1, LANES), lambda bi, h, qi: (0, 0))],
        out_specs=pl.BlockSpec((None, tq, LANES), lambda bi, h, qi: (bi, qi, h)),
        out_shape=jax.ShapeDtypeStruct((b, SEQ, C), BF16),
        compiler_params=_params("parallel", "parallel", "parallel"),
        name="diff_attn",
    )(lam, qkv, qkv, qkv, subln_gain)


def _gqa_kernel(q_ref, k_ref, v_ref, o_ref, *, tq):
    k = k_ref[...]
    v = v_ref[...]
    n = ATTN_SUB // 2
    m = _lo_mask((n, LANES))
    tiles = [(slice(t * n, (t + 1) * n), slice(half * LANES, (half + 1) * LANES))
             for t in range(tq // n) for half in range(2)]

    def scores(tile):
        return _dot_nt(_stack_heads(q_ref[tile]), k)

    s_next = scores(tiles[0])
    for i, tile in enumerate(tiles):
        s = s_next
        if i + 1 < len(tiles):
            s_next = scores(tiles[i + 1])
        p, inv = _softmax_rows(s)
        o = _dot(p.astype(BF16), v) * inv
        o_ref[tile] = jnp.where(m, o[:n], o[n:]).astype(BF16)


def _gqa(qkv, b, tq=512):
    kern = functools.partial(_gqa_kernel, tq=tq)
    k0 = C // LANES
    return pl.pallas_call(
        kern,
        grid=(b, GQ_KV, SEQ // tq),
        in_specs=[pl.BlockSpec((None, tq, 2 * LANES), lambda bi, j, qi: (bi, qi, j)),
                  pl.BlockSpec((None, SEQ, LANES), lambda bi, j, qi: (bi, 0, k0 + j)),
                  pl.BlockSpec((None, SEQ, LANES), lambda bi, j, qi: (bi, 0, k0 + GQ_KV + j))],
        out_specs=pl.BlockSpec((None, tq, 2 * LANES), lambda bi, j, qi: (bi, qi, j)),
        out_shape=jax.ShapeDtypeStruct((b, SEQ, C), BF16),
        compiler_params=_params("parallel", "parallel", "parallel"),
        name="gqa_attn",
    )(qkv, qkv, qkv)


NA_TQ = NA_QROWS * GRID_W
NA_TK = NA_KROWS * GRID_W
NA_KBLK = NA_TK // NA_TQ
NA_STEPS = SEQ // NA_TQ
NA_SUB = 512
NA_BATCH = 8


def _na_kernel(q_ref, k0_ref, k1_ref, k2_ref, v0_ref, v1_ref, v2_ref, bias_ref, o_ref, *, nb):
    n = NA_SUB // 2
    m = _lo_mask((n, LANES))
    tiles = [(bi, t) for bi in range(nb) for t in range(NA_TQ // n)]

    def scores(tile):
        bi, t = tile
        k = jnp.concatenate([k0_ref[bi], k1_ref[bi], k2_ref[bi]], axis=0)
        return _dot_nt(_stack_heads(q_ref[bi, t * n:(t + 1) * n, :]), k)

    s_next = scores(tiles[0])
    for i, (bi, t) in enumerate(tiles):
        s = s_next
        if i + 1 < len(tiles):
            s_next = scores(tiles[i + 1])
        rows = slice(t * n, (t + 1) * n)
        bias = jnp.concatenate([bias_ref[rows, :],
                                bias_ref[NA_TQ + t * n:NA_TQ + (t + 1) * n, :]], axis=0)
        p, inv = _softmax_rows(s + bias)
        v = jnp.concatenate([v0_ref[bi], v1_ref[bi], v2_ref[bi]], axis=0)
        o = _dot(p.astype(BF16), v) * inv
        o_ref[bi, rows, :] = jnp.where(m, o[:n], o[n:]).astype(BF16)


def _na_key_block(i):
    return jnp.clip(i - 1, 0, NA_STEPS - NA_KBLK)


def _na_pattern(i):
    return jnp.minimum(i, 2) + i // (NA_STEPS - 1)


def _na_attn(qkv, bias, b):
    nblk = C // LANES
    nb = math.gcd(b, NA_BATCH)

    def kv_spec(base, j):
        return pl.BlockSpec((nb, NA_TQ, LANES),
                            lambda i, hp, bi: (bi, _na_key_block(i) + j, base + hp))

    return pl.pallas_call(
        functools.partial(_na_kernel, nb=nb),
        grid=(NA_STEPS, nblk, b // nb),
        in_specs=[pl.BlockSpec((nb, NA_TQ, LANES), lambda i, hp, bi: (bi, i, hp))]
        + [kv_spec(nblk, j) for j in range(NA_KBLK)]
        + [kv_spec(2 * nblk, j) for j in range(NA_KBLK)]
        + [pl.BlockSpec((None, None, 2 * NA_TQ, NA_TK),
                        lambda i, hp, bi: (_na_pattern(i), hp, 0, 0))],
        out_specs=pl.BlockSpec((nb, NA_TQ, LANES), lambda i, hp, bi: (bi, i, hp)),
        out_shape=jax.ShapeDtypeStruct((b, SEQ, C), BF16),
        compiler_params=_params("parallel", "parallel", "parallel"),
        name="na_attn",
    )(qkv, qkv, qkv, qkv, qkv, qkv, qkv, bias)


def _na_bias_table(rel_bias):
    rows = SEQ // GRID_W
    wr, wc = NA_WIN_ROWS, NA_WIN_COLS
    heads = rel_bias.shape[0]
    cols = np.arange(GRID_W)
    col_start = np.clip(cols - wc // 2, 0, GRID_W - wc)
    col_mask = (cols[None, :] >= col_start[:, None]) & (cols[None, :] < col_start[:, None] + wc)
    dc = np.clip(cols[None, :] - cols[:, None] + (wc - 1), 0, 2 * wc - 2)
    onehot = (dc.reshape(-1)[None, :] == np.arange(2 * wc - 1)[:, None]).astype(np.float32)
    tiles = jnp.dot(rel_bias.reshape(heads * (2 * wr - 1), 2 * wc - 1), jnp.asarray(onehot),
                    precision=lax.Precision.HIGHEST)
    tiles = tiles.reshape(heads, 2 * wr - 1, GRID_W, GRID_W)
    tiles = jnp.where(jnp.asarray(col_mask)[None, None], tiles * LOG2E, -1e30)
    tiles = jnp.concatenate([tiles, jnp.full((heads, 1, GRID_W, GRID_W), -1e30, F32)], axis=1)
    n_dr = 2 * wr - 1
    idx = np.full((4, NA_QROWS, NA_KROWS), n_dr, np.int32)
    starts = [(0, 0), (NA_QROWS, 0), (2 * NA_QROWS, NA_QROWS), (rows - NA_QROWS, rows - NA_KROWS)]
    for p, (r0, ks) in enumerate(starts):
        for a in range(NA_QROWS):
            r = r0 + a
            rs = min(max(r - wr // 2, 0), rows - wr)
            for bb in range(NA_KROWS):
                j = ks + bb
                if rs <= j < rs + wr:
                    idx[p, a, bb] = j - r + (wr - 1)
    big = tiles[:, jnp.asarray(idx)]
    big = big.transpose(1, 0, 2, 4, 3, 5)
    return big.reshape(4, heads // 2, 2 * NA_TQ, NA_TK)


def _rwkv_pre_kernel(x_ref, xp_ref, xn_ref, g_ref, mu_ref, wr_ref, wk_ref, wv_ref, g1_ref, g2_ref,
                     w1_ref, w2_ref, a1_ref, a2_ref, w0_ref, a0_ref, kk_ref, ka_ref,
                     r_o, v_o, g_o, kk_o, kd0_o, kd1_o, ag0_o, ag1_o, lw0_o, lw1_o, *,
                     tiles_per_seq):
    g = g_ref[...]
    h = _rms(x_ref[...], g, NORM_EPS)
    hh = _rms(_halo_rows(xp_ref, xn_ref, tiles_per_seq), g, NORM_EPS)
    up, dn = _shift_rows(h, hh[HALO - 1:HALO], hh[HALO:HALO + 1])
    xx = 0.5 * (up + dn) - h

    def mix(j):
        return (h + xx * mu_ref[j:j + 1, :]).astype(BF16)

    tw = jnp.tanh(_dot(mix(1), w1_ref[...]))
    ta = _dot(mix(4), a1_ref[...])
    gate = jax.nn.sigmoid(_dot(mix(5), g1_ref[...]))
    lo = _lo_mask(tw.shape)
    sels = (lo, jnp.logical_not(lo))
    zs = [_dot(jnp.where(s, tw, 0.0).astype(BF16), w2_ref[...]) for s in sels]
    az = [_dot(jnp.where(s, ta, 0.0).astype(BF16), a2_ref[...]) for s in sels]
    k = _dot(mix(2), wk_ref[...])
    r_o[...] = _dot(mix(0), wr_ref[...]).astype(BF16)
    v_o[...] = _dot(mix(3), wv_ref[...]).astype(BF16)
    g_o[...] = _dot(gate.astype(BF16), g2_ref[...]).astype(BF16)

    kkv = k * kk_ref[...]
    nrm = jnp.sqrt(_head_sum(kkv * kkv, _head_ones(COL_CHUNK)))
    kk_o[...] = (kkv / jnp.maximum(nrm, 1e-12)).astype(BF16)
    ka = ka_ref[...]
    for d, (kd_o, ag_o, lw_o) in enumerate(((kd0_o, ag0_o, lw0_o), (kd1_o, ag1_o, lw1_o))):
        lw_o[...] = -math.exp(-0.5) * jax.nn.sigmoid(w0_ref[d:d + 1, :] + zs[d])
        a = jax.nn.sigmoid(a0_ref[d:d + 1, :] + az[d])
        ag_o[...] = a.astype(BF16)
        kd_o[...] = (k * (1.0 + (a - 1.0) * ka)).astype(BF16)


def _rwkv_pre(x, g, mu, w_r, w_k, w_v, g1, g2, k_k, k_a, w0, w1, w2, a0, a1, a2, tm=512):
    t = x.shape[0]
    lora = w1.shape[-1]
    w1c = jnp.concatenate([w1[0], w1[1]], axis=1).astype(BF16)
    a1c = jnp.concatenate([a1[0], a1[1]], axis=1).astype(BF16)
    w2c = w2.reshape(2 * lora, C).astype(BF16)
    a2c = a2.reshape(2 * lora, C).astype(BF16)
    assert 2 * lora == LANES
    glora = g1.shape[-1]
    row = pl.BlockSpec((tm, C), lambda i: (i, 0))
    kern = functools.partial(_rwkv_pre_kernel, tiles_per_seq=SEQ // tm)
    bf = jax.ShapeDtypeStruct((t, C), BF16)
    f32 = jax.ShapeDtypeStruct((t, C), F32)
    return pl.pallas_call(
        kern,
        grid=(t // tm,),
        in_specs=[row] + _halo_specs(tm, t) + [
            _const_spec((1, C)), _const_spec((6, C)),
            _const_spec((C, C)), _const_spec((C, C)), _const_spec((C, C)),
            _const_spec((C, glora)), _const_spec((glora, C)),
            _const_spec((C, LANES)), _const_spec((LANES, C)),
            _const_spec((C, LANES)), _const_spec((LANES, C)),
            _const_spec((2, C)), _const_spec((2, C)), _const_spec((1, C)), _const_spec((1, C))],
        out_specs=[row] * 10,
        out_shape=[bf] * 8 + [f32] * 2,
        compiler_params=_params("parallel"),
        name="rwkv_pre",
    )(x, x, x, g.reshape(1, C), mu, w_r.astype(BF16), w_k.astype(BF16), w_v.astype(BF16),
      g1.astype(BF16), g2.astype(BF16), w1c, w2c, a1c, a2c, w0, a0,
      k_k.reshape(1, C), k_a.reshape(1, C))


WKV_GROUP = 16


def _wkv_kernel(r_ref, v_ref, kk_ref, kd0_ref, kd1_ref, ag0_ref, ag1_ref, lw0_ref, lw1_ref,
                y_ref, *scratch, chunk, n_chunks):
    n2 = 2 * chunk
    ri = lax.broadcasted_iota(jnp.int32, (n2, 2 * n2), 0)
    ci = lax.broadcasted_iota(jnp.int32, (n2, 2 * n2), 1)
    same_head = (ri // chunk) == ((ci % n2) // chunk)
    tr, tc = ri % chunk, ci % chunk
    qi = lax.broadcasted_iota(jnp.int32, (n2, n2), 0)
    qj = lax.broadcasted_iota(jnp.int32, (n2, n2), 1)
    eye = jnp.where(qi == qj, 1.0, 0.0)
    level_masks = [(qi // 2) == (qj // 2)]
    size = 2
    while size < chunk:
        level_masks.append(((qi // (2 * size)) == (qj // (2 * size)))
                           & ((qi // size) != (qj // size)))
        size *= 2
    n_scr = len(scratch) // 2
    dirs = ((kd0_ref, ag0_ref, lw0_ref) + tuple(scratch[:n_scr]),
            (kd1_ref, ag1_ref, lw1_ref) + tuple(scratch[n_scr:]))

    def chunk_prefix(x):
        pos = lax.broadcasted_iota(jnp.int32, x.shape, 0) % chunk
        shift = 1
        while shift < chunk:
            x = x + jnp.where(pos >= shift, pltpu.roll(x, shift, 0), 0.0)
            shift *= 2
        return x

    masks = ((same_head & (tr > tc), same_head & (tr >= tc)),
             (same_head & (tr < tc), same_head & (tr <= tc)))

    grp = WKV_GROUP
    grows = grp * chunk

    def bmm(a, b):
        return jnp.einsum('gmk,gkn->gmn', a, b, preferred_element_type=F32)


    def phase1(g, carry):
        rows = pl.ds(pl.multiple_of(g * grows, grows), grows)
        slab = pl.ds(pl.multiple_of(g * grp, grp), grp)
        pairs = pl.ds(pl.multiple_of(g * (grp // 2), grp // 2), grp // 2)

        def load(ref):
            return ref[rows, :].reshape(grp, chunk, LANES)

        r = load(r_ref).astype(F32)
        kk = load(kk_ref).astype(F32)
        v_s = _stack_heads(load(v_ref))
        v_pad = jnp.concatenate([jnp.zeros((grp, n2, LANES), BF16), v_s], axis=2)
        y_loc = []
        for d, (kd_ref, ag_ref, lw_ref, rt_ref, mc_ref, dl_ref, dec_ref,
                mc2_ref, dl2_ref, dec2_ref) in enumerate(dirs):
            strict, incl = masks[d]
            lw = lw_ref[rows, :]
            pre = chunk_prefix(lw).reshape(grp, chunk, LANES)
            lw = lw.reshape(grp, chunk, LANES)
            tot = pre[:, chunk - 1:chunk, :]
            if d == 0:
                c_in = pre
                c_ex = pre - lw
            else:
                c_ex = tot - pre
                c_in = c_ex + lw
            kd = load(kd_ref).astype(F32)
            b = kk * load(ag_ref).astype(F32)
            e_neg = jnp.exp(-c_in)
            e_end = jnp.exp(tot - c_in)
            a_t = _stack_heads(-kk * jnp.exp(c_ex))
            r_t = _stack_heads(r * jnp.exp(c_in))
            b_t = (b * e_neg).astype(BF16)
            k_t = (kd * e_neg).astype(BF16)
            lhs = jnp.concatenate([a_t, r_t], axis=1).astype(BF16)
            rhs = jnp.concatenate([b_t, b_t, k_t, k_t], axis=1)
            gram = jnp.einsum('gmd,gnd->gmn', lhs, rhs, preferred_element_type=F32)
            a_abk = jnp.where(strict, gram[:, :n2, :], 0.0)
            a_ab = a_abk[:, :, :n2]
            a_ak = a_abk[:, :, n2:].astype(BF16)
            m_rbk = jnp.where(incl, gram[:, n2:, :], 0.0).astype(BF16)
            inv = eye + jnp.where(level_masks[0], a_ab, 0.0)
            for lvl in range(1, len(level_masks)):
                off = jnp.where(level_masks[lvl], a_ab, 0.0).astype(BF16)
                inv_bf = inv.astype(BF16)
                inv = inv + bmm(bmm(inv_bf, off).astype(BF16), inv_bf)
            x0 = jnp.concatenate([a_t.astype(BF16), bmm(a_ak, v_s).astype(BF16)], axis=2)
            wu = bmm(inv.astype(BF16), x0)
            q = jnp.concatenate([wu.astype(BF16), v_pad], axis=1)
            ry = bmm(m_rbk, q)
            rt_ref[slab] = (r_t + ry[:, :, :LANES]).astype(BF16)
            y_loc.append(ry[:, :chunk, LANES:] + ry[:, chunk:, LANES:])
            bk = jnp.concatenate([_stack_heads(b * e_end), _stack_heads(kd * e_end)],
                                 axis=1).astype(BF16)
            md = jnp.einsum('gkm,gkn->gmn', q, bk, preferred_element_type=F32)
            mc = md[:, :LANES, :]
            dl = md[:, LANES:, :]
            dec = jnp.exp(tot)
            mc_ref[slab] = mc.astype(BF16)
            dl_ref[slab] = dl
            dec_ref[slab] = jnp.broadcast_to(dec, (grp, 8, LANES))
            first, second = (0, 1) if d == 0 else (1, 0)

            def pick(a, which):
                return a.reshape((grp // 2, 2) + a.shape[1:])[:, which]

            dec_a, dec_b = pick(dec, first), pick(dec, second)
            f1 = pick(mc, first) + eye * dec_a
            fd = jnp.concatenate([f1, pick(dl, first)], axis=1).astype(BF16)
            prod = bmm(fd, pick(mc, second).astype(BF16))
            mc2_ref[pairs] = (prod[:, :LANES, :] + pick(mc, first) * dec_b).astype(BF16)
            dl2_ref[pairs] = prod[:, LANES:, :] + pick(dl, first) * dec_b + pick(dl, second)
            dec2_ref[pairs] = jnp.broadcast_to(dec_a * dec_b, (grp // 2, 8, LANES))
        y_ref[rows, :] = (y_loc[0] + y_loc[1]).reshape(grows, LANES)
        return carry

    lax.fori_loop(0, n_chunks // grp, phase1, 0)

    n_pairs = n_chunks // 2

    def add_y(rt_ref, c, s_bf):
        rows = pl.ds(pl.multiple_of(c * chunk, chunk), chunk)
        y = _dot_nt(rt_ref[c], s_bf)
        y_ref[rows, :] = y_ref[rows, :] + (y[:chunk] + y[chunk:])

    def second_chunk(d, pair):
        return 2 * pair + 1 if d == 0 else 2 * pair

    def phase2(step, carry):
        new = []
        for d, (*_ins, rt_ref, mc_ref, dl_ref, dec_ref, mc2_ref, dl2_ref, dec2_ref) in enumerate(dirs):
            pair = step if d == 0 else n_pairs - 1 - step
            prev = jnp.maximum(step - 1, 0) if d == 0 else jnp.minimum(n_pairs - step, n_pairs - 1)
            c1 = 2 * pair if d == 0 else 2 * pair + 1
            s0, mid_prev = carry[d]
            s0_bf = s0.astype(BF16)
            s_next = s0 * dec2_ref[pair][0:1, :] + _dot(s0_bf, mc2_ref[pair]) + dl2_ref[pair]
            mid = s0 * dec_ref[c1][0:1, :] + _dot(s0_bf, mc_ref[c1]) + dl_ref[c1]
            add_y(rt_ref, c1, s0_bf)
            add_y(rt_ref, second_chunk(d, prev), mid_prev)
            new.append((s_next, mid.astype(BF16)))
        return tuple(new)

    zero = jnp.zeros((LANES, LANES), F32)
    init = (zero, zero.astype(BF16))
    final = lax.fori_loop(0, n_pairs, phase2, (init, init))
    for d, (*_ins, rt_ref, _mc, _dl, _dec, _mc2, _dl2, _dec2) in enumerate(dirs):
        last_pair = n_pairs - 1 if d == 0 else 0
        add_y(rt_ref, second_chunk(d, last_pair), final[d][1])


def _wkv(r, v, kk, kd0, kd1, ag0, ag1, lw0, lw1, b):
    spec = pl.BlockSpec((None, SEQ, LANES), lambda bi, hp: (bi, 0, hp))
    args = [a.reshape(b, SEQ, C) for a in (r, v, kk, kd0, kd1, ag0, ag1, lw0, lw1)]
    n_chunks = SEQ // WKV_CHUNK
    kern = functools.partial(_wkv_kernel, chunk=WKV_CHUNK, n_chunks=n_chunks)
    per_dir = [pltpu.VMEM((n_chunks, LANES, LANES), BF16),
               pltpu.VMEM((n_chunks, LANES, LANES), BF16),
               pltpu.VMEM((n_chunks, LANES, LANES), F32),
               pltpu.VMEM((n_chunks, 8, LANES), F32),
               pltpu.VMEM((n_chunks // 2, LANES, LANES), BF16),
               pltpu.VMEM((n_chunks // 2, LANES, LANES), F32),
               pltpu.VMEM((n_chunks // 2, 8, LANES), F32)]
    scratch = per_dir + per_dir
    return pl.pallas_call(
        kern,
        grid=(b, C // LANES),
        in_specs=[spec] * 9,
        out_specs=spec,
        out_shape=jax.ShapeDtypeStruct((b, SEQ, C), F32),
        scratch_shapes=scratch,
        compiler_params=_params("parallel", "parallel"),
        name="wkv7",
    )(*args)


def _rwkv_post_kernel(y_ref, r_ref, kd0_ref, kd1_ref, v_ref, g_ref, x_ref, lng_ref, lnb_ref,
                      rk_ref, wo_ref, o_ref):
    ones = _head_ones(COL_CHUNK)
    y = y_ref[...]
    dev = y - _head_sum(y, ones) * (1.0 / HD)
    var = _head_sum(dev * dev, ones) * (1.0 / HD)
    yn = dev * lax.rsqrt(var + RW_GN_EPS) * lng_ref[...] + lnb_ref[...]
    k_bonus = 0.5 * (kd0_ref[...].astype(F32) + kd1_ref[...].astype(F32))
    bonus = _head_sum(r_ref[...].astype(F32) * k_bonus * rk_ref[...], ones) * v_ref[...].astype(F32)
    out = ((yn + bonus) * g_ref[...].astype(F32)).astype(BF16)
    o_ref[...] = x_ref[...] + _dot(out, wo_ref[...])


def _rwkv_post(y, r, kd0, kd1, v, g, x, ln_g, ln_b, r_k, w_o, tm=256):
    t = x.shape[0]
    row = pl.BlockSpec((tm, C), lambda i: (i, 0))
    vec = _const_spec((1, C))
    return pl.pallas_call(
        _rwkv_post_kernel,
        grid=(t // tm,),
        in_specs=[row] * 7 + [vec, vec, vec, _const_spec((C, C))],
        out_specs=row,
        out_shape=jax.ShapeDtypeStruct((t, C), F32),
        compiler_params=_params("parallel"),
        name="rwkv_post",
    )(y, r, kd0, kd1, v, g, x, ln_g.reshape(1, C), ln_b.reshape(1, C), r_k.reshape(1, C),
      w_o.astype(BF16))


def _lambda_init(layer_idx):
    return 0.8 - 0.6 * math.exp(-0.3 * layer_idx)


def _head_gain(q_gain, n_q, k_gain, n_k, n_v):
    scale = HD ** -0.5 * LOG2E
    return jnp.concatenate([jnp.tile(q_gain * scale, n_q // HD), jnp.tile(k_gain, n_k // HD),
                            jnp.ones((n_v,), F32)])


def _encode(x, b, p):
    t = b * SEQ
    x = x.reshape(t, C)
    pos = jnp.arange(SEQ, dtype=F32)

    hn = _head_gain(p['a_q_norm'], C, p['a_k_norm'], C, C)
    qkv = _proj(x, p['n0_attn'], p['a_w_qkv'].astype(BF16), hn, 2 * C,
                rope=_rope_tables(pos, None, HD // 2), rope_half=HD // 2)
    lam_init = _lambda_init(0)
    lam = (jnp.exp(jnp.sum(p['a_lq1'] * p['a_lk1'])) - jnp.exp(jnp.sum(p['a_lq2'] * p['a_lk2']))
           + lam_init).reshape(1)
    o = _diff_attn(qkv.reshape(b, SEQ, 3 * C), lam,
                   (p['a_subln'] * (1.0 - lam_init)).reshape(1, LANES), b)
    x = _out_proj(o.reshape(t, C), p['a_w_o'].astype(BF16), x)
    x = _ffn(x, p['n0_ffn'], p['f0_w_in'], p['f0_conv_w'], p['f0_conv_b'], p['f0_w_out'])

    kvw = GQ_KV * HD
    w = p['b_w_qkv']

    def dup(wc):
        wc = wc.reshape(C, GQ_KV, 1, HD)
        return jnp.broadcast_to(wc, (C, GQ_KV, 2, HD)).reshape(C, 2 * kvw)

    w_dup = jnp.concatenate([w[:, :C], dup(w[:, C:C + kvw]), dup(w[:, C + kvw:])], axis=1)
    hn = _head_gain(p['b_q_norm'], C, p['b_k_norm'], 2 * kvw, 2 * kvw)
    tok = jnp.arange(SEQ)
    rope = _rope_tables((tok // GRID_W).astype(F32), (tok % GRID_W).astype(F32), HD // 4)
    qkv = _proj(x, p['n1_attn'], w_dup.astype(BF16), hn, C + 2 * kvw, rope=rope,
                rope_half=HD // 4)
    o = _gqa(qkv.reshape(b, SEQ, C + 4 * kvw), b)
    x = _out_proj(o.reshape(t, C), p['b_w_o'].astype(BF16), x)
    x = _ffn(x, p['n1_ffn'], p['f1_w_in'], p['f1_conv_w'], p['f1_conv_b'], p['f1_w_out'])

    hn = _head_gain(p['c_q_norm'], C, p['c_k_norm'], C, C)
    qkv = _proj(x, p['n2_attn'], p['c_w_qkv'].astype(BF16), hn, 2 * C)
    o = _na_attn(qkv.reshape(b, SEQ, 3 * C), _na_bias_table(p['c_rel_bias']), b)
    x = _out_proj(o.reshape(t, C), p['c_w_o'].astype(BF16), x)
    x = _ffn(x, p['n2_ffn'], p['f2_w_in'], p['f2_conv_w'], p['f2_conv_b'], p['f2_w_out'])

    r, v, g, kk, kd0, kd1, ag0, ag1, lw0, lw1 = _rwkv_pre(
        x, p['n3_attn'], p['d_mu'], p['d_w_r'], p['d_w_k'], p['d_w_v'], p['d_g1'], p['d_g2'],
        p['d_k_k'], p['d_k_a'], p['d_w0'], p['d_w1'], p['d_w2'], p['d_a0'], p['d_a1'], p['d_a2'])
    y = _wkv(r, v, kk, kd0, kd1, ag0, ag1, lw0, lw1, b)
    x = _rwkv_post(y.reshape(t, C), r, kd0, kd1, v, g, x, p['d_ln_g'], p['d_ln_b'], p['d_r_k'],
                   p['d_w_o'])
    x = _ffn(x, p['n3_ffn'], p['f3_w_in'], p['f3_conv_w'], p['f3_conv_b'], p['f3_w_out'])
    return x.reshape(b, SEQ, C)


def kernel(x_prompt, x_sample,
           n0_attn, n0_ffn, n1_attn, n1_ffn, n2_attn, n2_ffn, n3_attn, n3_ffn, a_w_qkv,
           a_q_norm, a_k_norm, a_lq1, a_lk1, a_lq2, a_lk2, a_subln, a_w_o, b_w_qkv,
           b_q_norm, b_k_norm, b_w_o, c_w_qkv, c_q_norm, c_k_norm, c_rel_bias, c_w_o, d_mu,
           d_w_r, d_w_k, d_w_v, d_w_o, d_g1, d_g2, d_k_k, d_k_a, d_r_k, d_ln_g, d_ln_b,
           d_w0, d_w1, d_w2, d_a0, d_a1, d_a2, f0_w_in, f0_conv_w, f0_conv_b, f0_w_out,
           f1_w_in, f1_conv_w, f1_conv_b, f1_w_out, f2_w_in, f2_conv_w, f2_conv_b, f2_w_out,
           f3_w_in, f3_conv_w, f3_conv_b, f3_w_out):
    p = dict(locals())
    xs = (p.pop('x_prompt'), p.pop('x_sample'))
    return tuple(_encode(x, x.shape[0], p) for x in xs)
```

```python
import functools
import math

import numpy as np
import jax
import jax.numpy as jnp
from jax import lax
from jax.experimental import pallas as pl
from jax.experimental.pallas import tpu as pltpu

F32 = jnp.float32
BF16 = jnp.bfloat16

C = 1024
HD = 64
LANES = 128
SEQ = 2048
GRID_W = 64
ROPE_THETA = 10000.0
NORM_EPS = 1e-6
DA_HEADS = 8
DA_SUBLN_EPS = 1e-5
GQ_KV = 4
NA_WIN_ROWS = 8
NA_WIN_COLS = 16
NA_QROWS = 4
NA_KROWS = 12
RW_GN_EPS = 64e-5
FFN_HIDDEN = 2816
WKV_CHUNK = 64
LOG2E = math.log2(math.e)

COL_CHUNK = 256
VMEM_LIMIT = 56 * 1024 * 1024


def _params(*sem):
    return pltpu.CompilerParams(dimension_semantics=sem, vmem_limit_bytes=VMEM_LIMIT)


def _rms(x, g, eps):
    return x * lax.rsqrt(jnp.mean(x * x, axis=-1, keepdims=True) + eps) * g


def _split(x):
    hi = x.astype(BF16)
    lo = (x - hi.astype(F32)).astype(BF16)
    return hi, lo


def _head_ones(n):
    r = lax.broadcasted_iota(jnp.int32, (n, n), 0)
    c = lax.broadcasted_iota(jnp.int32, (n, n), 1)
    return jnp.where((r // HD) == (c // HD), 1.0, 0.0).astype(BF16)


def _head_sum(x, ones):
    w = ones.shape[0]
    out = []
    for c in range(x.shape[1] // w):
        hi, lo = _split(x[:, c * w:(c + 1) * w])
        out.append(_dot(hi, ones) + _dot(lo, ones))
    return out[0] if len(out) == 1 else jnp.concatenate(out, axis=1)


def _dot(a, b):
    return jnp.dot(a, b, preferred_element_type=F32)


def _dot_nt(a, b):
    return lax.dot_general(a, b, (((1,), (1,)), ((), ())), preferred_element_type=F32)


def _shift_rows(u, prev_row, next_row):
    n = u.shape[0]
    up = pltpu.roll(u, 1, 0)
    dn = pltpu.roll(u, n - 1, 0)
    row = lax.broadcasted_iota(jnp.int32, (8, u.shape[1]), 0)
    up = jnp.concatenate([jnp.where(row == 0, prev_row, up[:8]), up[8:]], axis=0)
    dn = jnp.concatenate([dn[:n - 8], jnp.where(row == 7, next_row, dn[n - 8:])], axis=0)
    return up, dn


def _lo_mask(shape):
    return lax.broadcasted_iota(jnp.int32, shape, len(shape) - 1) < HD


def _stack_heads(x):
    m = _lo_mask(x.shape)
    zero = jnp.zeros_like(x)
    return jnp.concatenate([jnp.where(m, x, zero), jnp.where(m, zero, x)], axis=-2)


def _proj_kernel(*refs, n_chunks, n_norm_chunks, rope_half):
    if rope_half:
        x_ref, g_ref, w_ref, hn_ref, cos_ref, sin_ref, o_ref = refs
    else:
        x_ref, g_ref, w_ref, hn_ref, o_ref = refs
    xn = _rms(x_ref[...], g_ref[...], NORM_EPS).astype(BF16)
    ones = _head_ones(COL_CHUNK)
    if rope_half:
        lane = lax.broadcasted_iota(jnp.int32, (1, COL_CHUNK), 1)
        first = (lane % (2 * rope_half)) < rope_half
    def mm(c):
        return _dot(xn, w_ref[:, c * COL_CHUNK:(c + 1) * COL_CHUNK])

    y_next = mm(0)
    for c in range(n_chunks):
        cols = slice(c * COL_CHUNK, (c + 1) * COL_CHUNK)
        y = y_next
        if c + 1 < n_chunks:
            y_next = mm(c + 1)
        if c < n_norm_chunks:
            ms = _dot((y * y).astype(BF16), ones) * (1.0 / HD)
            y = y * lax.rsqrt(ms + NORM_EPS) * hn_ref[:, cols]
            if rope_half:
                rot = jnp.where(first, pltpu.roll(y, COL_CHUNK - rope_half, 1),
                                pltpu.roll(y, rope_half, 1))
                y = y * cos_ref[...] + rot * sin_ref[...]
        o_ref[:, cols] = y.astype(BF16)


def _proj(x, g, w, hn, n_norm_cols, rope=None, rope_half=0, tm=512):
    t = x.shape[0]
    n = w.shape[1]
    tiles_per_seq = SEQ // tm
    in_specs = [
        pl.BlockSpec((tm, C), lambda i: (i, 0)),
        pl.BlockSpec((1, C), lambda i: (0, 0)),
        pl.BlockSpec((C, n), lambda i: (0, 0)),
        pl.BlockSpec((1, n), lambda i: (0, 0)),
    ]
    args = [x, g.reshape(1, C), w, hn.reshape(1, n)]
    if rope_half:
        spec = pl.BlockSpec((tm, COL_CHUNK), lambda i: (i % tiles_per_seq, 0))
        in_specs += [spec, spec]
        args += [rope[0], rope[1]]
    kern = functools.partial(_proj_kernel, n_chunks=n // COL_CHUNK,
                             n_norm_chunks=n_norm_cols // COL_CHUNK, rope_half=rope_half)
    return pl.pallas_call(
        kern,
        grid=(t // tm,),
        in_specs=in_specs,
        out_specs=pl.BlockSpec((tm, n), lambda i: (i, 0)),
        out_shape=jax.ShapeDtypeStruct((t, n), BF16),
        compiler_params=_params("parallel"),
        name="proj",
    )(*args)


def _rope_tables(pos_first, pos_second, half):
    inv = ROPE_THETA ** (-jnp.arange(half, dtype=F32) / half)

    def one(pos):
        ang = pos[:, None] * inv[None, :]
        cos = jnp.concatenate([jnp.cos(ang), jnp.cos(ang)], axis=-1)
        sin = jnp.concatenate([-jnp.sin(ang), jnp.sin(ang)], axis=-1)
        return cos, sin

    c1, s1 = one(pos_first)
    if pos_second is None:
        cos, sin = c1, s1
    else:
        c2, s2 = one(pos_second)
        cos = jnp.concatenate([c1, c2], axis=-1)
        sin = jnp.concatenate([s1, s2], axis=-1)
    reps = COL_CHUNK // cos.shape[-1]
    return jnp.tile(cos, (1, reps)), jnp.tile(sin, (1, reps))


def _out_kernel(a_ref, w_ref, x_ref, o_ref):
    o_ref[...] = x_ref[...] + _dot(a_ref[...], w_ref[...])


def _out_proj(a, w, x, tm=1024):
    t = x.shape[0]
    return pl.pallas_call(
        _out_kernel,
        grid=(t // tm,),
        in_specs=[pl.BlockSpec((tm, C), lambda i: (i, 0)),
                  pl.BlockSpec((C, C), lambda i: (0, 0)),
                  pl.BlockSpec((tm, C), lambda i: (i, 0))],
        out_specs=pl.BlockSpec((tm, C), lambda i: (i, 0)),
        out_shape=jax.ShapeDtypeStruct((t, C), F32),
        compiler_params=_params("parallel"),
        name="out_proj",
    )(a, w, x)


HALO = 8


def _halo_rows(xp_ref, xn_ref, tiles_per_seq):
    i = pl.program_id(0)
    pos = i % tiles_per_seq
    halo = jnp.concatenate([xp_ref[...], xn_ref[...]], axis=0)
    row = lax.broadcasted_iota(jnp.int32, halo.shape, 0)
    has_prev = (pos != 0).astype(jnp.int32)
    has_next = (pos != tiles_per_seq - 1).astype(jnp.int32)
    keep = jnp.where(row < HALO, has_prev, has_next)
    return jnp.where(keep != 0, halo, 0.0)


def _ffn_kernel(x_ref, xp_ref, xn_ref, g_ref, wi_ref, cw_ref, cb_ref, wo_ref, o_ref, hid_ref, *,
                tm, tiles_per_seq):
    x = x_ref[...]
    xe = jnp.concatenate([x, _halo_rows(xp_ref, xn_ref, tiles_per_seq)], axis=0)
    xne = _rms(xe, g_ref[...], NORM_EPS).astype(BF16)

    def in_proj(c):
        gate_cols = slice(c * COL_CHUNK, (c + 1) * COL_CHUNK)
        val_cols = slice(FFN_HIDDEN + c * COL_CHUNK, FFN_HIDDEN + (c + 1) * COL_CHUNK)
        return _dot(xne, wi_ref[:, gate_cols]), _dot(xne, wi_ref[:, val_cols])

    def conv(u, cols):
        um = u[:tm]
        up, dn = _shift_rows(um, u[tm + HALO - 1:tm + HALO], u[tm + HALO:tm + HALO + 1])
        cw = cw_ref[:, cols]
        return up * cw[0:1] + um * cw[1:2] + dn * cw[2:3] + cb_ref[:, cols]

    n_chunks = FFN_HIDDEN // COL_CHUNK
    u_next = in_proj(0)
    for c in range(n_chunks):
        ug, uv = u_next
        if c + 1 < n_chunks:
            u_next = in_proj(c + 1)
        cols = slice(c * COL_CHUNK, (c + 1) * COL_CHUNK)
        gate = conv(ug, cols)
        val = conv(uv, slice(FFN_HIDDEN + c * COL_CHUNK, FFN_HIDDEN + (c + 1) * COL_CHUNK))
        hid_ref[:, cols] = (gate * jax.nn.sigmoid(gate) * val).astype(BF16)
    o_ref[...] = x + _dot(hid_ref[...], wo_ref[...])


def _halo_specs(tm, t):
    per = tm // HALO
    last = t // HALO - 1
    return [pl.BlockSpec((HALO, C), lambda i: (jnp.maximum(i * per - 1, 0), 0)),
            pl.BlockSpec((HALO, C), lambda i: (jnp.minimum((i + 1) * per, last), 0))]


def _const_spec(shape):
    return pl.BlockSpec(shape, lambda i: (0,) * len(shape), pipeline_mode=pl.Buffered(1))


def _ffn(x, g, w_in, conv_w, conv_b, w_out, tm=512):
    t = x.shape[0]
    kern = functools.partial(_ffn_kernel, tm=tm, tiles_per_seq=SEQ // tm)
    return pl.pallas_call(
        kern,
        grid=(t // tm,),
        in_specs=[pl.BlockSpec((tm, C), lambda i: (i, 0))] + _halo_specs(tm, t) + [
            _const_spec((1, C)),
            _const_spec((C, 2 * FFN_HIDDEN)),
            _const_spec((3, 2 * FFN_HIDDEN)),
            _const_spec((1, 2 * FFN_HIDDEN)),
            _const_spec((FFN_HIDDEN, C))],
        out_specs=pl.BlockSpec((tm, C), lambda i: (i, 0)),
        out_shape=jax.ShapeDtypeStruct((t, C), F32),
        scratch_shapes=[pltpu.VMEM((tm, FFN_HIDDEN), BF16)],
        compiler_params=_params("parallel"),
        name="conv_ffn",
    )(x, x, x, g.reshape(1, C), w_in.astype(BF16), conv_w, conv_b.reshape(1, -1),
      w_out.astype(BF16))


ATTN_SUB = 256


def _softmax_rows(s):
    p = jnp.exp2(s - jnp.max(s, axis=-1, keepdims=True))
    return p, 1.0 / jnp.sum(p, axis=-1, keepdims=True)


def _diff_attn_kernel(lam_ref, q_ref, k_ref, v_ref, sg_ref, o_ref, *, tq):
    k = k_ref[...]
    v = v_ref[...]
    n = ATTN_SUB // 2
    n_tiles = tq // n

    def scores(t):
        return _dot_nt(_stack_heads(q_ref[t * n:(t + 1) * n, :]), k)

    s_next = scores(0)
    for t in range(n_tiles):
        s = s_next
        if t + 1 < n_tiles:
            s_next = scores(t + 1)
        p, inv = _softmax_rows(s)
        o2 = _dot(p.astype(BF16), v) * inv
        o = o2[:n] - lam_ref[0] * o2[n:]
        o_ref[t * n:(t + 1) * n, :] = _rms(o, sg_ref[...], DA_SUBLN_EPS).astype(BF16)


def _diff_attn(qkv, lam, subln_gain, b, tq=2048):
    q_blocks = C // LANES
    kern = functools.partial(_diff_attn_kernel, tq=tq)
    return pl.pallas_call(
        kern,
        grid=(b, DA_HEADS, SEQ // tq),
        in_specs=[pl.BlockSpec(memory_space=pltpu.SMEM),
                  pl.BlockSpec((None, tq, LANES), lambda bi, h, qi: (bi, qi, h)),
                  pl.BlockSpec((None, SEQ, LANES), lambda bi, h, qi: (bi, 0, q_blocks + h)),
                  pl.BlockSpec((None, SEQ, LANES), lambda bi, h, qi: (bi, 0, 2 * q_blocks + h)),
                  pl.BlockSpec((1, LANES), lambda bi, h, qi: (0, 0))],
        out_specs=pl.BlockSpec((None, tq, LANES), lambda bi, h, qi: (bi, qi, h)),
        out_shape=jax.ShapeDtypeStruct((b, SEQ, C), BF16),
        compiler_params=_params("parallel", "parallel", "parallel"),
        name="diff_attn",
    )(lam, qkv, qkv, qkv, subln_gain)


def _gqa_kernel(q_ref, k_ref, v_ref, o_ref, *, tq):
    k = k_ref[...]
    v = v_ref[...]
    n = ATTN_SUB // 2
    m = _lo_mask((n, LANES))
    tiles = [(slice(t * n, (t + 1) * n), slice(half * LANES, (half + 1) * LANES))
             for t in range(tq // n) for half in range(2)]

    def scores(tile):
        return _dot_nt(_stack_heads(q_ref[tile]), k)

    s_next = scores(tiles[0])
    for i, tile in enumerate(tiles):
        s = s_next
        if i + 1 < len(tiles):
            s_next = scores(tiles[i + 1])
        p, inv = _softmax_rows(s)
        o = _dot(p.astype(BF16), v) * inv
        o_ref[tile] = jnp.where(m, o[:n], o[n:]).astype(BF16)


def _gqa(qkv, b, tq=1024):
    kern = functools.partial(_gqa_kernel, tq=tq)
    k0 = C // LANES
    return pl.pallas_call(
        kern,
        grid=(b, GQ_KV, SEQ // tq),
        in_specs=[pl.BlockSpec((None, tq, 2 * LANES), lambda bi, j, qi: (bi, qi, j)),
                  pl.BlockSpec((None, SEQ, LANES), lambda bi, j, qi: (bi, 0, k0 + j)),
                  pl.BlockSpec((None, SEQ, LANES), lambda bi, j, qi: (bi, 0, k0 + GQ_KV + j))],
        out_specs=pl.BlockSpec((None, tq, 2 * LANES), lambda bi, j, qi: (bi, qi, j)),
        out_shape=jax.ShapeDtypeStruct((b, SEQ, C), BF16),
        compiler_params=_params("parallel", "parallel", "parallel"),
        name="gqa_attn",
    )(qkv, qkv, qkv)


NA_TQ = NA_QROWS * GRID_W
NA_TK = NA_KROWS * GRID_W
NA_KBLK = NA_TK // NA_TQ
NA_STEPS = SEQ // NA_TQ
NA_SUB = 512
NA_BATCH = 8


def _na_kernel(q_ref, k0_ref, k1_ref, k2_ref, v0_ref, v1_ref, v2_ref, bias_ref, o_ref, *, nb):
    n = NA_SUB // 2
    m = _lo_mask((n, LANES))
    tiles = [(bi, t) for bi in range(nb) for t in range(NA_TQ // n)]

    def scores(tile):
        bi, t = tile
        k = jnp.concatenate([k0_ref[bi], k1_ref[bi], k2_ref[bi]], axis=0)
        return _dot_nt(_stack_heads(q_ref[bi, t * n:(t + 1) * n, :]), k)

    s_next = scores(tiles[0])
    for i, (bi, t) in enumerate(tiles):
        s = s_next
        if i + 1 < len(tiles):
            s_next = scores(tiles[i + 1])
        rows = slice(t * n, (t + 1) * n)
        bias = jnp.concatenate([bias_ref[rows, :],
                                bias_ref[NA_TQ + t * n:NA_TQ + (t + 1) * n, :]], axis=0)
        p, inv = _softmax_rows(s + bias)
        v = jnp.concatenate([v0_ref[bi], v1_ref[bi], v2_ref[bi]], axis=0)
        o = _dot(p.astype(BF16), v) * inv
        o_ref[bi, rows, :] = jnp.where(m, o[:n], o[n:]).astype(BF16)


def _na_key_block(i):
    return jnp.clip(i - 1, 0, NA_STEPS - NA_KBLK)


def _na_pattern(i):
    return jnp.minimum(i, 2) + i // (NA_STEPS - 1)


def _na_attn(qkv, bias, b):
    nblk = C // LANES
    nb = math.gcd(b, NA_BATCH)

    def kv_spec(base, j):
        return pl.BlockSpec((nb, NA_TQ, LANES),
                            lambda i, hp, bi: (bi, _na_key_block(i) + j, base + hp))

    return pl.pallas_call(
        functools.partial(_na_kernel, nb=nb),
        grid=(NA_STEPS, nblk, b // nb),
        in_specs=[pl.BlockSpec((nb, NA_TQ, LANES), lambda i, hp, bi: (bi, i, hp))]
        + [kv_spec(nblk, j) for j in range(NA_KBLK)]
        + [kv_spec(2 * nblk, j) for j in range(NA_KBLK)]
        + [pl.BlockSpec((None, None, 2 * NA_TQ, NA_TK),
                        lambda i, hp, bi: (_na_pattern(i), hp, 0, 0))],
        out_specs=pl.BlockSpec((nb, NA_TQ, LANES), lambda i, hp, bi: (bi, i, hp)),
        out_shape=jax.ShapeDtypeStruct((b, SEQ, C), BF16),
        compiler_params=_params("parallel", "parallel", "parallel"),
        name="na_attn",
    )(qkv, qkv, qkv, qkv, qkv, qkv, qkv, bias)


def _na_bias_table(rel_bias):
    rows = SEQ // GRID_W
    wr, wc = NA_WIN_ROWS, NA_WIN_COLS
    heads = rel_bias.shape[0]
    cols = np.arange(GRID_W)
    col_start = np.clip(cols - wc // 2, 0, GRID_W - wc)
    col_mask = (cols[None, :] >= col_start[:, None]) & (cols[None, :] < col_start[:, None] + wc)
    dc = np.clip(cols[None, :] - cols[:, None] + (wc - 1), 0, 2 * wc - 2)
    onehot = (dc.reshape(-1)[None, :] == np.arange(2 * wc - 1)[:, None]).astype(np.float32)
    tiles = jnp.dot(rel_bias.reshape(heads * (2 * wr - 1), 2 * wc - 1), jnp.asarray(onehot),
                    precision=lax.Precision.HIGHEST)
    tiles = tiles.reshape(heads, 2 * wr - 1, GRID_W, GRID_W)
    tiles = jnp.where(jnp.asarray(col_mask)[None, None], tiles * LOG2E, -1e30)
    tiles = jnp.concatenate([tiles, jnp.full((heads, 1, GRID_W, GRID_W), -1e30, F32)], axis=1)
    n_dr = 2 * wr - 1
    idx = np.full((4, NA_QROWS, NA_KROWS), n_dr, np.int32)
    starts = [(0, 0), (NA_QROWS, 0), (2 * NA_QROWS, NA_QROWS), (rows - NA_QROWS, rows - NA_KROWS)]
    for p, (r0, ks) in enumerate(starts):
        for a in range(NA_QROWS):
            r = r0 + a
            rs = min(max(r - wr // 2, 0), rows - wr)
            for bb in range(NA_KROWS):
                j = ks + bb
                if rs <= j < rs + wr:
                    idx[p, a, bb] = j - r + (wr - 1)
    big = tiles[:, jnp.asarray(idx)]
    big = big.transpose(1, 0, 2, 4, 3, 5)
    return big.reshape(4, heads // 2, 2 * NA_TQ, NA_TK)


def _rwkv_pre_kernel(x_ref, xp_ref, xn_ref, g_ref, mu_ref, wr_ref, wk_ref, wv_ref, g1_ref, g2_ref,
                     w1_ref, w2_ref, a1_ref, a2_ref, w0_ref, a0_ref, kk_ref, ka_ref,
                     r_o, v_o, g_o, kk_o, kd0_o, kd1_o, ag0_o, ag1_o, lw0_o, lw1_o, *,
                     tiles_per_seq):
    g = g_ref[...]
    h = _rms(x_ref[...], g, NORM_EPS)
    hh = _rms(_halo_rows(xp_ref, xn_ref, tiles_per_seq), g, NORM_EPS)
    up, dn = _shift_rows(h, hh[HALO - 1:HALO], hh[HALO:HALO + 1])
    xx = 0.5 * (up + dn) - h

    def mix(j):
        return (h + xx * mu_ref[j:j + 1, :]).astype(BF16)

    tw = jnp.tanh(_dot(mix(1), w1_ref[...]))
    ta = _dot(mix(4), a1_ref[...])
    gate = jax.nn.sigmoid(_dot(mix(5), g1_ref[...]))
    lo = _lo_mask(tw.shape)
    sels = (lo, jnp.logical_not(lo))
    zs = [_dot(jnp.where(s, tw, 0.0).astype(BF16), w2_ref[...]) for s in sels]
    az = [_dot(jnp.where(s, ta, 0.0).astype(BF16), a2_ref[...]) for s in sels]
    k = _dot(mix(2), wk_ref[...])
    r_o[...] = _dot(mix(0), wr_ref[...]).astype(BF16)
    v_o[...] = _dot(mix(3), wv_ref[...]).astype(BF16)
    g_o[...] = _dot(gate.astype(BF16), g2_ref[...]).astype(BF16)

    kkv = k * kk_ref[...]
    nrm = jnp.sqrt(_head_sum(kkv * kkv, _head_ones(COL_CHUNK)))
    kk_o[...] = (kkv / jnp.maximum(nrm, 1e-12)).astype(BF16)
    ka = ka_ref[...]
    for d, (kd_o, ag_o, lw_o) in enumerate(((kd0_o, ag0_o, lw0_o), (kd1_o, ag1_o, lw1_o))):
        lw_o[...] = -math.exp(-0.5) * jax.nn.sigmoid(w0_ref[d:d + 1, :] + zs[d])
        a = jax.nn.sigmoid(a0_ref[d:d + 1, :] + az[d])
        ag_o[...] = a.astype(BF16)
        kd_o[...] = (k * (1.0 + (a - 1.0) * ka)).astype(BF16)


def _rwkv_pre(x, g, mu, w_r, w_k, w_v, g1, g2, k_k, k_a, w0, w1, w2, a0, a1, a2, tm=512):
    t = x.shape[0]
    lora = w1.shape[-1]
    w1c = jnp.concatenate([w1[0], w1[1]], axis=1).astype(BF16)
    a1c = jnp.concatenate([a1[0], a1[1]], axis=1).astype(BF16)
    w2c = w2.reshape(2 * lora, C).astype(BF16)
    a2c = a2.reshape(2 * lora, C).astype(BF16)
    assert 2 * lora == LANES
    glora = g1.shape[-1]
    row = pl.BlockSpec((tm, C), lambda i: (i, 0))
    kern = functools.partial(_rwkv_pre_kernel, tiles_per_seq=SEQ // tm)
    bf = jax.ShapeDtypeStruct((t, C), BF16)
    f32 = jax.ShapeDtypeStruct((t, C), F32)
    return pl.pallas_call(
        kern,
        grid=(t // tm,),
        in_specs=[row] + _halo_specs(tm, t) + [
            _const_spec((1, C)), _const_spec((6, C)),
            _const_spec((C, C)), _const_spec((C, C)), _const_spec((C, C)),
            _const_spec((C, glora)), _const_spec((glora, C)),
            _const_spec((C, LANES)), _const_spec((LANES, C)),
            _const_spec((C, LANES)), _const_spec((LANES, C)),
            _const_spec((2, C)), _const_spec((2, C)), _const_spec((1, C)), _const_spec((1, C))],
        out_specs=[row] * 10,
        out_shape=[bf] * 8 + [f32] * 2,
        compiler_params=_params("parallel"),
        name="rwkv_pre",
    )(x, x, x, g.reshape(1, C), mu, w_r.astype(BF16), w_k.astype(BF16), w_v.astype(BF16),
      g1.astype(BF16), g2.astype(BF16), w1c, w2c, a1c, a2c, w0, a0,
      k_k.reshape(1, C), k_a.reshape(1, C))


WKV_GROUP = 16


def _wkv_kernel(r_ref, v_ref, kk_ref, kd0_ref, kd1_ref, ag0_ref, ag1_ref, lw0_ref, lw1_ref,
                y_ref, *scratch, chunk, n_chunks):
    n2 = 2 * chunk
    ri = lax.broadcasted_iota(jnp.int32, (n2, 2 * n2), 0)
    ci = lax.broadcasted_iota(jnp.int32, (n2, 2 * n2), 1)
    same_head = (ri // chunk) == ((ci % n2) // chunk)
    tr, tc = ri % chunk, ci % chunk
    qi = lax.broadcasted_iota(jnp.int32, (n2, n2), 0)
    qj = lax.broadcasted_iota(jnp.int32, (n2, n2), 1)
    eye = jnp.where(qi == qj, 1.0, 0.0)
    level_masks = [(qi // 2) == (qj // 2)]
    size = 2
    while size < chunk:
        level_masks.append(((qi // (2 * size)) == (qj // (2 * size)))
                           & ((qi // size) != (qj // size)))
        size *= 2
    n_scr = len(scratch) // 2
    dirs = ((kd0_ref, ag0_ref, lw0_ref) + tuple(scratch[:n_scr]),
            (kd1_ref, ag1_ref, lw1_ref) + tuple(scratch[n_scr:]))

    def chunk_prefix(x):
        pos = lax.broadcasted_iota(jnp.int32, x.shape, 0) % chunk
        shift = 1
        while shift < chunk:
            x = x + jnp.where(pos >= shift, pltpu.roll(x, shift, 0), 0.0)
            shift *= 2
        return x

    masks = ((same_head & (tr > tc), same_head & (tr >= tc)),
             (same_head & (tr < tc), same_head & (tr <= tc)))

    grp = WKV_GROUP
    grows = grp * chunk

    def bmm(a, b):
        return jnp.einsum('gmk,gkn->gmn', a, b, preferred_element_type=F32)


    def phase1(g, carry):
        rows = pl.ds(pl.multiple_of(g * grows, grows), grows)
        slab = pl.ds(pl.multiple_of(g * grp, grp), grp)
        pairs = pl.ds(pl.multiple_of(g * (grp // 2), grp // 2), grp // 2)

        def load(ref):
            return ref[rows, :].reshape(grp, chunk, LANES)

        r = load(r_ref).astype(F32)
        kk = load(kk_ref).astype(F32)
        v_s = _stack_heads(load(v_ref))
        v_pad = jnp.concatenate([jnp.zeros((grp, n2, LANES), BF16), v_s], axis=2)
        y_loc = []
        for d, (kd_ref, ag_ref, lw_ref, rt_ref, mc_ref, dl_ref, dec_ref,
                mc2_ref, dl2_ref, dec2_ref) in enumerate(dirs):
            strict, incl = masks[d]
            lw = lw_ref[rows, :]
            pre = chunk_prefix(lw).reshape(grp, chunk, LANES)
            lw = lw.reshape(grp, chunk, LANES)
            tot = pre[:, chunk - 1:chunk, :]
            if d == 0:
                c_in = pre
                c_ex = pre - lw
            else:
                c_ex = tot - pre
                c_in = c_ex + lw
            kd = load(kd_ref).astype(F32)
            b = kk * load(ag_ref).astype(F32)
            e_neg = jnp.exp(-c_in)
            e_end = jnp.exp(tot - c_in)
            a_t = _stack_heads(-kk * jnp.exp(c_ex))
            r_t = _stack_heads(r * jnp.exp(c_in))
            b_t = (b * e_neg).astype(BF16)
            k_t = (kd * e_neg).astype(BF16)
            lhs = jnp.concatenate([a_t, r_t], axis=1).astype(BF16)
            rhs = jnp.concatenate([b_t, b_t, k_t, k_t], axis=1)
            gram = jnp.einsum('gmd,gnd->gmn', lhs, rhs, preferred_element_type=F32)
            a_abk = jnp.where(strict, gram[:, :n2, :], 0.0)
            a_ab = a_abk[:, :, :n2]
            a_ak = a_abk[:, :, n2:].astype(BF16)
            m_rbk = jnp.where(incl, gram[:, n2:, :], 0.0).astype(BF16)
            inv = eye + jnp.where(level_masks[0], a_ab, 0.0)
            for lvl in range(1, len(level_masks)):
                off = jnp.where(level_masks[lvl], a_ab, 0.0).astype(BF16)
                inv_bf = inv.astype(BF16)
                inv = inv + bmm(bmm(inv_bf, off).astype(BF16), inv_bf)
            x0 = jnp.concatenate([a_t.astype(BF16), bmm(a_ak, v_s).astype(BF16)], axis=2)
            wu = bmm(inv.astype(BF16), x0)
            q = jnp.concatenate([wu.astype(BF16), v_pad], axis=1)
            ry = bmm(m_rbk, q)
            rt_ref[slab] = (r_t + ry[:, :, :LANES]).astype(BF16)
            y_loc.append(ry[:, :chunk, LANES:] + ry[:, chunk:, LANES:])
            bk = jnp.concatenate([_stack_heads(b * e_end), _stack_heads(kd * e_end)],
                                 axis=1).astype(BF16)
            md = jnp.einsum('gkm,gkn->gmn', q, bk, preferred_element_type=F32)
            mc = md[:, :LANES, :]
            dl = md[:, LANES:, :]
            dec = jnp.exp(tot)
            mc_ref[slab] = mc.astype(BF16)
            dl_ref[slab] = dl
            dec_ref[slab] = jnp.broadcast_to(dec, (grp, 8, LANES))
            first, second = (0, 1) if d == 0 else (1, 0)

            def pick(a, which):
                return a.reshape((grp // 2, 2) + a.shape[1:])[:, which]

            dec_a, dec_b = pick(dec, first), pick(dec, second)
            f1 = pick(mc, first) + eye * dec_a
            fd = jnp.concatenate([f1, pick(dl, first)], axis=1).astype(BF16)
            prod = bmm(fd, pick(mc, second).astype(BF16))
            mc2_ref[pairs] = (prod[:, :LANES, :] + pick(mc, first) * dec_b).astype(BF16)
            dl2_ref[pairs] = prod[:, LANES:, :] + pick(dl, first) * dec_b + pick(dl, second)
            dec2_ref[pairs] = jnp.broadcast_to(dec_a * dec_b, (grp // 2, 8, LANES))
        y_ref[rows, :] = (y_loc[0] + y_loc[1]).reshape(grows, LANES)
        return carry

    lax.fori_loop(0, n_chunks // grp, phase1, 0)

    n_pairs = n_chunks // 2

    def add_y(rt_ref, c, s_bf):
        rows = pl.ds(pl.multiple_of(c * chunk, chunk), chunk)
        y = _dot_nt(rt_ref[c], s_bf)
        y_ref[rows, :] = y_ref[rows, :] + (y[:chunk] + y[chunk:])

    def second_chunk(d, pair):
        return 2 * pair + 1 if d == 0 else 2 * pair

    def phase2(step, carry):
        new = []
        for d, (*_ins, rt_ref, mc_ref, dl_ref, dec_ref, mc2_ref, dl2_ref, dec2_ref) in enumerate(dirs):
            pair = step if d == 0 else n_pairs - 1 - step
            prev = jnp.maximum(step - 1, 0) if d == 0 else jnp.minimum(n_pairs - step, n_pairs - 1)
            c1 = 2 * pair if d == 0 else 2 * pair + 1
            s0, mid_prev = carry[d]
            s0_bf = s0.astype(BF16)
            s_next = s0 * dec2_ref[pair][0:1, :] + _dot(s0_bf, mc2_ref[pair]) + dl2_ref[pair]
            mid = s0 * dec_ref[c1][0:1, :] + _dot(s0_bf, mc_ref[c1]) + dl_ref[c1]
            add_y(rt_ref, c1, s0_bf)
            add_y(rt_ref, second_chunk(d, prev), mid_prev)
            new.append((s_next, mid.astype(BF16)))
        return tuple(new)

    zero = jnp.zeros((LANES, LANES), F32)
    init = (zero, zero.astype(BF16))
    final = lax.fori_loop(0, n_pairs, phase2, (init, init), unroll=8)
    for d, (*_ins, rt_ref, _mc, _dl, _dec, _mc2, _dl2, _dec2) in enumerate(dirs):
        last_pair = n_pairs - 1 if d == 0 else 0
        add_y(rt_ref, second_chunk(d, last_pair), final[d][1])


def _wkv(r, v, kk, kd0, kd1, ag0, ag1, lw0, lw1, b):
    spec = pl.BlockSpec((None, SEQ, LANES), lambda bi, hp: (bi, 0, hp))
    args = [a.reshape(b, SEQ, C) for a in (r, v, kk, kd0, kd1, ag0, ag1, lw0, lw1)]
    n_chunks = SEQ // WKV_CHUNK
    kern = functools.partial(_wkv_kernel, chunk=WKV_CHUNK, n_chunks=n_chunks)
    per_dir = [pltpu.VMEM((n_chunks, LANES, LANES), BF16),
               pltpu.VMEM((n_chunks, LANES, LANES), BF16),
               pltpu.VMEM((n_chunks, LANES, LANES), F32),
               pltpu.VMEM((n_chunks, 8, LANES), F32),
               pltpu.VMEM((n_chunks // 2, LANES, LANES), BF16),
               pltpu.VMEM((n_chunks // 2, LANES, LANES), F32),
               pltpu.VMEM((n_chunks // 2, 8, LANES), F32)]
    scratch = per_dir + per_dir
    return pl.pallas_call(
        kern,
        grid=(b, C // LANES),
        in_specs=[spec] * 9,
        out_specs=spec,
        out_shape=jax.ShapeDtypeStruct((b, SEQ, C), F32),
        scratch_shapes=scratch,
        compiler_params=_params("parallel", "parallel"),
        name="wkv7",
    )(*args)


def _rwkv_post_kernel(y_ref, r_ref, kd0_ref, kd1_ref, v_ref, g_ref, x_ref, lng_ref, lnb_ref,
                      rk_ref, wo_ref, o_ref):
    ones = _head_ones(COL_CHUNK)
    y = y_ref[...]
    dev = y - _head_sum(y, ones) * (1.0 / HD)
    var = _head_sum(dev * dev, ones) * (1.0 / HD)
    yn = dev * lax.rsqrt(var + RW_GN_EPS) * lng_ref[...] + lnb_ref[...]
    k_bonus = 0.5 * (kd0_ref[...].astype(F32) + kd1_ref[...].astype(F32))
    bonus = _head_sum(r_ref[...].astype(F32) * k_bonus * rk_ref[...], ones) * v_ref[...].astype(F32)
    out = ((yn + bonus) * g_ref[...].astype(F32)).astype(BF16)
    o_ref[...] = x_ref[...] + _dot(out, wo_ref[...])


def _rwkv_post(y, r, kd0, kd1, v, g, x, ln_g, ln_b, r_k, w_o, tm=512):
    t = x.shape[0]
    row = pl.BlockSpec((tm, C), lambda i: (i, 0))
    vec = _const_spec((1, C))
    return pl.pallas_call(
        _rwkv_post_kernel,
        grid=(t // tm,),
        in_specs=[row] * 7 + [vec, vec, vec, _const_spec((C, C))],
        out_specs=row,
        out_shape=jax.ShapeDtypeStruct((t, C), F32),
        compiler_params=_params("parallel"),
        name="rwkv_post",
    )(y, r, kd0, kd1, v, g, x, ln_g.reshape(1, C), ln_b.reshape(1, C), r_k.reshape(1, C),
      w_o.astype(BF16))


def _lambda_init(layer_idx):
    return 0.8 - 0.6 * math.exp(-0.3 * layer_idx)


def _head_gain(q_gain, n_q, k_gain, n_k, n_v):
    scale = HD ** -0.5 * LOG2E
    return jnp.concatenate([jnp.tile(q_gain * scale, n_q // HD), jnp.tile(k_gain, n_k // HD),
                            jnp.ones((n_v,), F32)])


def _encode(x, b, p):
    t = b * SEQ
    x = x.reshape(t, C)
    pos = jnp.arange(SEQ, dtype=F32)

    hn = _head_gain(p['a_q_norm'], C, p['a_k_norm'], C, C)
    qkv = _proj(x, p['n0_attn'], p['a_w_qkv'].astype(BF16), hn, 2 * C,
                rope=_rope_tables(pos, None, HD // 2), rope_half=HD // 2)
    lam_init = _lambda_init(0)
    lam = (jnp.exp(jnp.sum(p['a_lq1'] * p['a_lk1'])) - jnp.exp(jnp.sum(p['a_lq2'] * p['a_lk2']))
           + lam_init).reshape(1)
    o = _diff_attn(qkv.reshape(b, SEQ, 3 * C), lam,
                   (p['a_subln'] * (1.0 - lam_init)).reshape(1, LANES), b)
    x = _out_proj(o.reshape(t, C), p['a_w_o'].astype(BF16), x)
    x = _ffn(x, p['n0_ffn'], p['f0_w_in'], p['f0_conv_w'], p['f0_conv_b'], p['f0_w_out'])

    kvw = GQ_KV * HD
    w = p['b_w_qkv']

    def dup(wc):
        wc = wc.reshape(C, GQ_KV, 1, HD)
        return jnp.broadcast_to(wc, (C, GQ_KV, 2, HD)).reshape(C, 2 * kvw)

    w_dup = jnp.concatenate([w[:, :C], dup(w[:, C:C + kvw]), dup(w[:, C + kvw:])], axis=1)
    hn = _head_gain(p['b_q_norm'], C, p['b_k_norm'], 2 * kvw, 2 * kvw)
    tok = jnp.arange(SEQ)
    rope = _rope_tables((tok // GRID_W).astype(F32), (tok % GRID_W).astype(F32), HD // 4)
    qkv = _proj(x, p['n1_attn'], w_dup.astype(BF16), hn, C + 2 * kvw, rope=rope,
                rope_half=HD // 4)
    o = _gqa(qkv.reshape(b, SEQ, C + 4 * kvw), b)
    x = _out_proj(o.reshape(t, C), p['b_w_o'].astype(BF16), x)
    x = _ffn(x, p['n1_ffn'], p['f1_w_in'], p['f1_conv_w'], p['f1_conv_b'], p['f1_w_out'])

    hn = _head_gain(p['c_q_norm'], C, p['c_k_norm'], C, C)
    qkv = _proj(x, p['n2_attn'], p['c_w_qkv'].astype(BF16), hn, 2 * C)
    o = _na_attn(qkv.reshape(b, SEQ, 3 * C), _na_bias_table(p['c_rel_bias']), b)
    x = _out_proj(o.reshape(t, C), p['c_w_o'].astype(BF16), x)
    x = _ffn(x, p['n2_ffn'], p['f2_w_in'], p['f2_conv_w'], p['f2_conv_b'], p['f2_w_out'])

    r, v, g, kk, kd0, kd1, ag0, ag1, lw0, lw1 = _rwkv_pre(
        x, p['n3_attn'], p['d_mu'], p['d_w_r'], p['d_w_k'], p['d_w_v'], p['d_g1'], p['d_g2'],
        p['d_k_k'], p['d_k_a'], p['d_w0'], p['d_w1'], p['d_w2'], p['d_a0'], p['d_a1'], p['d_a2'])
    y = _wkv(r, v, kk, kd0, kd1, ag0, ag1, lw0, lw1, b)
    x = _rwkv_post(y.reshape(t, C), r, kd0, kd1, v, g, x, p['d_ln_g'], p['d_ln_b'], p['d_r_k'],
                   p['d_w_o'])
    x = _ffn(x, p['n3_ffn'], p['f3_w_in'], p['f3_conv_w'], p['f3_conv_b'], p['f3_w_out'])
    return x.reshape(b, SEQ, C)


def kernel(x_prompt, x_sample,
           n0_attn, n0_ffn, n1_attn, n1_ffn, n2_attn, n2_ffn, n3_attn, n3_ffn, a_w_qkv,
           a_q_norm, a_k_norm, a_lq1, a_lk1, a_lq2, a_lk2, a_subln, a_w_o, b_w_qkv,
           b_q_norm, b_k_norm, b_w_o, c_w_qkv, c_q_norm, c_k_norm, c_rel_bias, c_w_o, d_mu,
           d_w_r, d_w_k, d_w_v, d_w_o, d_g1, d_g2, d_k_k, d_k_a, d_r_k, d_ln_g, d_ln_b,
           d_w0, d_w1, d_w2, d_a0, d_a1, d_a2, f0_w_in, f0_conv_w, f0_conv_b, f0_w_out,
           f1_w_in, f1_conv_w, f1_conv_b, f1_w_out, f2_w_in, f2_conv_w, f2_conv_b, f2_w_out,
           f3_w_in, f3_conv_w, f3_conv_b, f3_w_out):
    p = dict(locals())
    xs = (p.pop('x_prompt'), p.pop('x_sample'))
    return tuple(_encode(x, x.shape[0], p) for x in xs)
```

```python
import functools
import math

import numpy as np
import jax
import jax.numpy as jnp
from jax import lax
from jax.experimental import pallas as pl
from jax.experimental.pallas import tpu as pltpu

F32 = jnp.float32
BF16 = jnp.bfloat16

C = 1024
HD = 64
LANES = 128
SEQ = 2048
GRID_W = 64
ROPE_THETA = 10000.0
NORM_EPS = 1e-6
DA_HEADS = 8
DA_SUBLN_EPS = 1e-5
GQ_KV = 4
NA_WIN_ROWS = 8
NA_WIN_COLS = 16
NA_QROWS = 4
NA_KROWS = 12
RW_GN_EPS = 64e-5
FFN_HIDDEN = 2816
WKV_CHUNK = 64
LOG2E = math.log2(math.e)

COL_CHUNK = 256
VMEM_LIMIT = 56 * 1024 * 1024


def _params(*sem):
    return pltpu.CompilerParams(dimension_semantics=sem, vmem_limit_bytes=VMEM_LIMIT)


def _rms(x, g, eps):
    return x * lax.rsqrt(jnp.mean(x * x, axis=-1, keepdims=True) + eps) * g


def _split(x):
    hi = x.astype(BF16)
    lo = (x - hi.astype(F32)).astype(BF16)
    return hi, lo


def _head_ones(n):
    r = lax.broadcasted_iota(jnp.int32, (n, n), 0)
    c = lax.broadcasted_iota(jnp.int32, (n, n), 1)
    return jnp.where((r // HD) == (c // HD), 1.0, 0.0).astype(BF16)


def _head_sum(x, ones):
    w = ones.shape[0]
    out = []
    for c in range(x.shape[1] // w):
        hi, lo = _split(x[:, c * w:(c + 1) * w])
        out.append(_dot(hi, ones) + _dot(lo, ones))
    return out[0] if len(out) == 1 else jnp.concatenate(out, axis=1)


def _dot(a, b):
    return jnp.dot(a, b, preferred_element_type=F32)


def _dot_nt(a, b):
    return lax.dot_general(a, b, (((1,), (1,)), ((), ())), preferred_element_type=F32)


def _shift_rows(u, prev_row, next_row):
    n = u.shape[0]
    up = pltpu.roll(u, 1, 0)
    dn = pltpu.roll(u, n - 1, 0)
    row = lax.broadcasted_iota(jnp.int32, (8, u.shape[1]), 0)
    up = jnp.concatenate([jnp.where(row == 0, prev_row, up[:8]), up[8:]], axis=0)
    dn = jnp.concatenate([dn[:n - 8], jnp.where(row == 7, next_row, dn[n - 8:])], axis=0)
    return up, dn


def _lo_mask(shape):
    return lax.broadcasted_iota(jnp.int32, shape, len(shape) - 1) < HD


def _stack_heads(x):
    m = _lo_mask(x.shape)
    zero = jnp.zeros_like(x)
    return jnp.concatenate([jnp.where(m, x, zero), jnp.where(m, zero, x)], axis=-2)


def _proj_kernel(*refs, n_chunks, n_norm_chunks, rope_half):
    if rope_half:
        x_ref, g_ref, w_ref, hn_ref, cos_ref, sin_ref, o_ref = refs
    else:
        x_ref, g_ref, w_ref, hn_ref, o_ref = refs
    xn = _rms(x_ref[...], g_ref[...], NORM_EPS).astype(BF16)
    ones = _head_ones(COL_CHUNK)
    if rope_half:
        lane = lax.broadcasted_iota(jnp.int32, (1, COL_CHUNK), 1)
        first = (lane % (2 * rope_half)) < rope_half
    def mm(c):
        return _dot(xn, w_ref[:, c * COL_CHUNK:(c + 1) * COL_CHUNK])

    y_next = mm(0)
    for c in range(n_chunks):
        cols = slice(c * COL_CHUNK, (c + 1) * COL_CHUNK)
        y = y_next
        if c + 1 < n_chunks:
            y_next = mm(c + 1)
        if c < n_norm_chunks:
            ms = _dot((y * y).astype(BF16), ones) * (1.0 / HD)
            y = y * lax.rsqrt(ms + NORM_EPS) * hn_ref[:, cols]
            if rope_half:
                rot = jnp.where(first, pltpu.roll(y, COL_CHUNK - rope_half, 1),
                                pltpu.roll(y, rope_half, 1))
                y = y * cos_ref[...] + rot * sin_ref[...]
        o_ref[:, cols] = y.astype(BF16)


def _proj(x, g, w, hn, n_norm_cols, rope=None, rope_half=0, tm=512):
    t = x.shape[0]
    n = w.shape[1]
    tiles_per_seq = SEQ // tm
    in_specs = [
        pl.BlockSpec((tm, C), lambda i: (i, 0)),
        pl.BlockSpec((1, C), lambda i: (0, 0)),
        pl.BlockSpec((C, n), lambda i: (0, 0)),
        pl.BlockSpec((1, n), lambda i: (0, 0)),
    ]
    args = [x, g.reshape(1, C), w, hn.reshape(1, n)]
    if rope_half:
        spec = pl.BlockSpec((tm, COL_CHUNK), lambda i: (i % tiles_per_seq, 0))
        in_specs += [spec, spec]
        args += [rope[0], rope[1]]
    kern = functools.partial(_proj_kernel, n_chunks=n // COL_CHUNK,
                             n_norm_chunks=n_norm_cols // COL_CHUNK, rope_half=rope_half)
    return pl.pallas_call(
        kern,
        grid=(t // tm,),
        in_specs=in_specs,
        out_specs=pl.BlockSpec((tm, n), lambda i: (i, 0)),
        out_shape=jax.ShapeDtypeStruct((t, n), BF16),
        compiler_params=_params("parallel"),
        name="proj",
    )(*args)


def _rope_tables(pos_first, pos_second, half):
    inv = ROPE_THETA ** (-jnp.arange(half, dtype=F32) / half)

    def one(pos):
        ang = pos[:, None] * inv[None, :]
        cos = jnp.concatenate([jnp.cos(ang), jnp.cos(ang)], axis=-1)
        sin = jnp.concatenate([-jnp.sin(ang), jnp.sin(ang)], axis=-1)
        return cos, sin

    c1, s1 = one(pos_first)
    if pos_second is None:
        cos, sin = c1, s1
    else:
        c2, s2 = one(pos_second)
        cos = jnp.concatenate([c1, c2], axis=-1)
        sin = jnp.concatenate([s1, s2], axis=-1)
    reps = COL_CHUNK // cos.shape[-1]
    return jnp.tile(cos, (1, reps)), jnp.tile(sin, (1, reps))


def _out_kernel(a_ref, w_ref, x_ref, o_ref):
    o_ref[...] = x_ref[...] + _dot(a_ref[...], w_ref[...])


def _out_proj(a, w, x, tm=1024):
    t = x.shape[0]
    return pl.pallas_call(
        _out_kernel,
        grid=(t // tm,),
        in_specs=[pl.BlockSpec((tm, C), lambda i: (i, 0)),
                  pl.BlockSpec((C, C), lambda i: (0, 0)),
                  pl.BlockSpec((tm, C), lambda i: (i, 0))],
        out_specs=pl.BlockSpec((tm, C), lambda i: (i, 0)),
        out_shape=jax.ShapeDtypeStruct((t, C), F32),
        compiler_params=_params("parallel"),
        name="out_proj",
    )(a, w, x)


HALO = 8


def _halo_rows(xp_ref, xn_ref, tiles_per_seq):
    i = pl.program_id(0)
    pos = i % tiles_per_seq
    halo = jnp.concatenate([xp_ref[...], xn_ref[...]], axis=0)
    row = lax.broadcasted_iota(jnp.int32, halo.shape, 0)
    has_prev = (pos != 0).astype(jnp.int32)
    has_next = (pos != tiles_per_seq - 1).astype(jnp.int32)
    keep = jnp.where(row < HALO, has_prev, has_next)
    return jnp.where(keep != 0, halo, 0.0)


def _ffn_kernel(x_ref, xp_ref, xn_ref, g_ref, wi_ref, cw_ref, cb_ref, wo_ref, o_ref, hid_ref, *,
                tm, tiles_per_seq):
    x = x_ref[...]
    xe = jnp.concatenate([x, _halo_rows(xp_ref, xn_ref, tiles_per_seq)], axis=0)
    xne = _rms(xe, g_ref[...], NORM_EPS).astype(BF16)

    def in_proj(c):
        gate_cols = slice(c * COL_CHUNK, (c + 1) * COL_CHUNK)
        val_cols = slice(FFN_HIDDEN + c * COL_CHUNK, FFN_HIDDEN + (c + 1) * COL_CHUNK)
        return _dot(xne, wi_ref[:, gate_cols]), _dot(xne, wi_ref[:, val_cols])

    def conv(u, cols):
        um = u[:tm]
        up, dn = _shift_rows(um, u[tm + HALO - 1:tm + HALO], u[tm + HALO:tm + HALO + 1])
        cw = cw_ref[:, cols]
        return up * cw[0:1] + um * cw[1:2] + dn * cw[2:3] + cb_ref[:, cols]

    n_chunks = FFN_HIDDEN // COL_CHUNK
    u_next = in_proj(0)
    for c in range(n_chunks):
        ug, uv = u_next
        if c + 1 < n_chunks:
            u_next = in_proj(c + 1)
        cols = slice(c * COL_CHUNK, (c + 1) * COL_CHUNK)
        gate = conv(ug, cols)
        val = conv(uv, slice(FFN_HIDDEN + c * COL_CHUNK, FFN_HIDDEN + (c + 1) * COL_CHUNK))
        hid_ref[:, cols] = (gate * jax.nn.sigmoid(gate) * val).astype(BF16)
    o_ref[...] = x + _dot(hid_ref[...], wo_ref[...])


def _halo_specs(tm, t):
    per = tm // HALO
    last = t // HALO - 1
    return [pl.BlockSpec((HALO, C), lambda i: (jnp.maximum(i * per - 1, 0), 0)),
            pl.BlockSpec((HALO, C), lambda i: (jnp.minimum((i + 1) * per, last), 0))]


def _const_spec(shape):
    return pl.BlockSpec(shape, lambda i: (0,) * len(shape), pipeline_mode=pl.Buffered(1))


def _ffn(x, g, w_in, conv_w, conv_b, w_out, tm=512):
    t = x.shape[0]
    kern = functools.partial(_ffn_kernel, tm=tm, tiles_per_seq=SEQ // tm)
    return pl.pallas_call(
        kern,
        grid=(t // tm,),
        in_specs=[pl.BlockSpec((tm, C), lambda i: (i, 0))] + _halo_specs(tm, t) + [
            _const_spec((1, C)),
            _const_spec((C, 2 * FFN_HIDDEN)),
            _const_spec((3, 2 * FFN_HIDDEN)),
            _const_spec((1, 2 * FFN_HIDDEN)),
            _const_spec((FFN_HIDDEN, C))],
        out_specs=pl.BlockSpec((tm, C), lambda i: (i, 0)),
        out_shape=jax.ShapeDtypeStruct((t, C), F32),
        scratch_shapes=[pltpu.VMEM((tm, FFN_HIDDEN), BF16)],
        compiler_params=_params("parallel"),
        name="conv_ffn",
    )(x, x, x, g.reshape(1, C), w_in.astype(BF16), conv_w, conv_b.reshape(1, -1),
      w_out.astype(BF16))


ATTN_SUB = 256


def _softmax_rows(s):
    p = jnp.exp2(s - jnp.max(s, axis=-1, keepdims=True))
    return p, 1.0 / jnp.sum(p, axis=-1, keepdims=True)


def _diff_attn_kernel(lam_ref, q_ref, k_ref, v_ref, sg_ref, o_ref, *, tq):
    k = k_ref[...]
    v = v_ref[...]
    n = ATTN_SUB // 2
    n_tiles = tq // n

    def scores(t):
        return _dot_nt(_stack_heads(q_ref[t * n:(t + 1) * n, :]), k)

    s_next = scores(0)
    for t in range(n_tiles):
        s = s_next
        if t + 1 < n_tiles:
            s_next = scores(t + 1)
        p, inv = _softmax_rows(s)
        o2 = _dot(p.astype(BF16), v) * inv
        o = o2[:n] - lam_ref[0] * o2[n:]
        o_ref[t * n:(t + 1) * n, :] = _rms(o, sg_ref[...], DA_SUBLN_EPS).astype(BF16)


def _diff_attn(qkv, lam, subln_gain, b, tq=2048):
    q_blocks = C // LANES
    kern = functools.partial(_diff_attn_kernel, tq=tq)
    return pl.pallas_call(
        kern,
        grid=(b, DA_HEADS, SEQ // tq),
        in_specs=[pl.BlockSpec(memory_space=pltpu.SMEM),
                  pl.BlockSpec((None, tq, LANES), lambda bi, h, qi: (bi, qi, h)),
                  pl.BlockSpec((None, SEQ, LANES), lambda bi, h, qi: (bi, 0, q_blocks + h)),
                  pl.BlockSpec((None, SEQ, LANES), lambda bi, h, qi: (bi, 0, 2 * q_blocks + h)),
                  pl.BlockSpec((1, LANES), lambda bi, h, qi: (0, 0))],
        out_specs=pl.BlockSpec((None, tq, LANES), lambda bi, h, qi: (bi, qi, h)),
        out_shape=jax.ShapeDtypeStruct((b, SEQ, C), BF16),
        compiler_params=_params("parallel", "parallel", "parallel"),
        name="diff_attn",
    )(lam, qkv, qkv, qkv, subln_gain)


def _gqa_kernel(q_ref, k_ref, v_ref, o_ref, *, tq):
    k = k_ref[...]
    v = v_ref[...]
    n = ATTN_SUB // 2
    m = _lo_mask((n, LANES))
    tiles = [(slice(t * n, (t + 1) * n), slice(half * LANES, (half + 1) * LANES))
             for t in range(tq // n) for half in range(2)]

    def scores(tile):
        return _dot_nt(_stack_heads(q_ref[tile]), k)

    s_next = scores(tiles[0])
    for i, tile in enumerate(tiles):
        s = s_next
        if i + 1 < len(tiles):
            s_next = scores(tiles[i + 1])
        p, inv = _softmax_rows(s)
        o = _dot(p.astype(BF16), v) * inv
        o_ref[tile] = jnp.where(m, o[:n], o[n:]).astype(BF16)


def _gqa(qkv, b, tq=1024):
    kern = functools.partial(_gqa_kernel, tq=tq)
    k0 = C // LANES
    return pl.pallas_call(
        kern,
        grid=(b, GQ_KV, SEQ // tq),
        in_specs=[pl.BlockSpec((None, tq, 2 * LANES), lambda bi, j, qi: (bi, qi, j)),
                  pl.BlockSpec((None, SEQ, LANES), lambda bi, j, qi: (bi, 0, k0 + j)),
                  pl.BlockSpec((None, SEQ, LANES), lambda bi, j, qi: (bi, 0, k0 + GQ_KV + j))],
        out_specs=pl.BlockSpec((None, tq, 2 * LANES), lambda bi, j, qi: (bi, qi, j)),
        out_shape=jax.ShapeDtypeStruct((b, SEQ, C), BF16),
        compiler_params=_params("parallel", "parallel", "parallel"),
        name="gqa_attn",
    )(qkv, qkv, qkv)


NA_TQ = NA_QROWS * GRID_W
NA_TK = NA_KROWS * GRID_W
NA_KBLK = NA_TK // NA_TQ
NA_STEPS = SEQ // NA_TQ
NA_SUB = 512
NA_BATCH = 8


def _na_kernel(q_ref, k0_ref, k1_ref, k2_ref, v0_ref, v1_ref, v2_ref, bias_ref, o_ref, *, nb):
    n = NA_SUB // 2
    m = _lo_mask((n, LANES))
    tiles = [(bi, t) for bi in range(nb) for t in range(NA_TQ // n)]

    def scores(tile):
        bi, t = tile
        k = jnp.concatenate([k0_ref[bi], k1_ref[bi], k2_ref[bi]], axis=0)
        return _dot_nt(_stack_heads(q_ref[bi, t * n:(t + 1) * n, :]), k)

    s_next = scores(tiles[0])
    for i, (bi, t) in enumerate(tiles):
        s = s_next
        if i + 1 < len(tiles):
            s_next = scores(tiles[i + 1])
        rows = slice(t * n, (t + 1) * n)
        bias = jnp.concatenate([bias_ref[rows, :],
                                bias_ref[NA_TQ + t * n:NA_TQ + (t + 1) * n, :]], axis=0)
        p, inv = _softmax_rows(s + bias)
        v = jnp.concatenate([v0_ref[bi], v1_ref[bi], v2_ref[bi]], axis=0)
        o = _dot(p.astype(BF16), v) * inv
        o_ref[bi, rows, :] = jnp.where(m, o[:n], o[n:]).astype(BF16)


def _na_key_block(i):
    return jnp.clip(i - 1, 0, NA_STEPS - NA_KBLK)


def _na_pattern(i):
    return jnp.minimum(i, 2) + i // (NA_STEPS - 1)


def _na_attn(qkv, bias, b):
    nblk = C // LANES
    nb = math.gcd(b, NA_BATCH)

    def kv_spec(base, j):
        return pl.BlockSpec((nb, NA_TQ, LANES),
                            lambda i, hp, bi: (bi, _na_key_block(i) + j, base + hp))

    return pl.pallas_call(
        functools.partial(_na_kernel, nb=nb),
        grid=(NA_STEPS, nblk, b // nb),
        in_specs=[pl.BlockSpec((nb, NA_TQ, LANES), lambda i, hp, bi: (bi, i, hp))]
        + [kv_spec(nblk, j) for j in range(NA_KBLK)]
        + [kv_spec(2 * nblk, j) for j in range(NA_KBLK)]
        + [pl.BlockSpec((None, None, 2 * NA_TQ, NA_TK),
                        lambda i, hp, bi: (_na_pattern(i), hp, 0, 0))],
        out_specs=pl.BlockSpec((nb, NA_TQ, LANES), lambda i, hp, bi: (bi, i, hp)),
        out_shape=jax.ShapeDtypeStruct((b, SEQ, C), BF16),
        compiler_params=_params("parallel", "parallel", "parallel"),
        name="na_attn",
    )(qkv, qkv, qkv, qkv, qkv, qkv, qkv, bias)


def _na_bias_table(rel_bias):
    rows = SEQ // GRID_W
    wr, wc = NA_WIN_ROWS, NA_WIN_COLS
    heads = rel_bias.shape[0]
    cols = np.arange(GRID_W)
    col_start = np.clip(cols - wc // 2, 0, GRID_W - wc)
    col_mask = (cols[None, :] >= col_start[:, None]) & (cols[None, :] < col_start[:, None] + wc)
    dc = np.clip(cols[None, :] - cols[:, None] + (wc - 1), 0, 2 * wc - 2)
    onehot = (dc.reshape(-1)[None, :] == np.arange(2 * wc - 1)[:, None]).astype(np.float32)
    tiles = jnp.dot(rel_bias.reshape(heads * (2 * wr - 1), 2 * wc - 1), jnp.asarray(onehot),
                    precision=lax.Precision.HIGHEST)
    tiles = tiles.reshape(heads, 2 * wr - 1, GRID_W, GRID_W)
    tiles = jnp.where(jnp.asarray(col_mask)[None, None], tiles * LOG2E, -1e30)
    tiles = jnp.concatenate([tiles, jnp.full((heads, 1, GRID_W, GRID_W), -1e30, F32)], axis=1)
    n_dr = 2 * wr - 1
    idx = np.full((4, NA_QROWS, NA_KROWS), n_dr, np.int32)
    starts = [(0, 0), (NA_QROWS, 0), (2 * NA_QROWS, NA_QROWS), (rows - NA_QROWS, rows - NA_KROWS)]
    for p, (r0, ks) in enumerate(starts):
        for a in range(NA_QROWS):
            r = r0 + a
            rs = min(max(r - wr // 2, 0), rows - wr)
            for bb in range(NA_KROWS):
                j = ks + bb
                if rs <= j < rs + wr:
                    idx[p, a, bb] = j - r + (wr - 1)
    big = tiles[:, jnp.asarray(idx)]
    big = big.transpose(1, 0, 2, 4, 3, 5)
    return big.reshape(4, heads // 2, 2 * NA_TQ, NA_TK)


def _rwkv_pre_kernel(x_ref, xp_ref, xn_ref, g_ref, mu_ref, wr_ref, wk_ref, wv_ref, g1_ref, g2_ref,
                     w1_ref, w2_ref, a1_ref, a2_ref, w0_ref, a0_ref, kk_ref, ka_ref,
                     r_o, v_o, g_o, kk_o, kd0_o, kd1_o, ag0_o, ag1_o, lw0_o, lw1_o, *,
                     tiles_per_seq):
    g = g_ref[...]
    h = _rms(x_ref[...], g, NORM_EPS)
    hh = _rms(_halo_rows(xp_ref, xn_ref, tiles_per_seq), g, NORM_EPS)
    up, dn = _shift_rows(h, hh[HALO - 1:HALO], hh[HALO:HALO + 1])
    xx = 0.5 * (up + dn) - h

    def mix(j):
        return (h + xx * mu_ref[j:j + 1, :]).astype(BF16)

    tw = jnp.tanh(_dot(mix(1), w1_ref[...]))
    ta = _dot(mix(4), a1_ref[...])
    gate = jax.nn.sigmoid(_dot(mix(5), g1_ref[...]))
    lo = _lo_mask(tw.shape)
    sels = (lo, jnp.logical_not(lo))
    zs = [_dot(jnp.where(s, tw, 0.0).astype(BF16), w2_ref[...]) for s in sels]
    az = [_dot(jnp.where(s, ta, 0.0).astype(BF16), a2_ref[...]) for s in sels]
    k = _dot(mix(2), wk_ref[...])
    r_o[...] = _dot(mix(0), wr_ref[...]).astype(BF16)
    v_o[...] = _dot(mix(3), wv_ref[...]).astype(BF16)
    g_o[...] = _dot(gate.astype(BF16), g2_ref[...]).astype(BF16)

    kkv = k * kk_ref[...]
    sq = _head_sum(kkv * kkv, _head_ones(COL_CHUNK))
    kk_o[...] = (kkv * lax.rsqrt(jnp.maximum(sq, 1e-24))).astype(BF16)
    ka = ka_ref[...]
    for d, (kd_o, ag_o, lw_o) in enumerate(((kd0_o, ag0_o, lw0_o), (kd1_o, ag1_o, lw1_o))):
        lw_o[...] = -math.exp(-0.5) * jax.nn.sigmoid(w0_ref[d:d + 1, :] + zs[d])
        a = jax.nn.sigmoid(a0_ref[d:d + 1, :] + az[d])
        ag_o[...] = a.astype(BF16)
        kd_o[...] = (k * (1.0 + (a - 1.0) * ka)).astype(BF16)


def _rwkv_pre(x, g, mu, w_r, w_k, w_v, g1, g2, k_k, k_a, w0, w1, w2, a0, a1, a2, tm=512):
    t = x.shape[0]
    lora = w1.shape[-1]
    w1c = jnp.concatenate([w1[0], w1[1]], axis=1).astype(BF16)
    a1c = jnp.concatenate([a1[0], a1[1]], axis=1).astype(BF16)
    w2c = w2.reshape(2 * lora, C).astype(BF16)
    a2c = a2.reshape(2 * lora, C).astype(BF16)
    assert 2 * lora == LANES
    glora = g1.shape[-1]
    row = pl.BlockSpec((tm, C), lambda i: (i, 0))
    kern = functools.partial(_rwkv_pre_kernel, tiles_per_seq=SEQ // tm)
    bf = jax.ShapeDtypeStruct((t, C), BF16)
    f32 = jax.ShapeDtypeStruct((t, C), F32)
    return pl.pallas_call(
        kern,
        grid=(t // tm,),
        in_specs=[row] + _halo_specs(tm, t) + [
            _const_spec((1, C)), _const_spec((6, C)),
            _const_spec((C, C)), _const_spec((C, C)), _const_spec((C, C)),
            _const_spec((C, glora)), _const_spec((glora, C)),
            _const_spec((C, LANES)), _const_spec((LANES, C)),
            _const_spec((C, LANES)), _const_spec((LANES, C)),
            _const_spec((2, C)), _const_spec((2, C)), _const_spec((1, C)), _const_spec((1, C))],
        out_specs=[row] * 10,
        out_shape=[bf] * 8 + [f32] * 2,
        compiler_params=_params("parallel"),
        name="rwkv_pre",
    )(x, x, x, g.reshape(1, C), mu, w_r.astype(BF16), w_k.astype(BF16), w_v.astype(BF16),
      g1.astype(BF16), g2.astype(BF16), w1c, w2c, a1c, a2c, w0, a0,
      k_k.reshape(1, C), k_a.reshape(1, C))


WKV_GROUP = 16


def _wkv_kernel(r_ref, v_ref, kk_ref, kd0_ref, kd1_ref, ag0_ref, ag1_ref, lw0_ref, lw1_ref,
                y_ref, *scratch, chunk, n_chunks):
    n2 = 2 * chunk
    ri = lax.broadcasted_iota(jnp.int32, (n2, 2 * n2), 0)
    ci = lax.broadcasted_iota(jnp.int32, (n2, 2 * n2), 1)
    same_head = (ri // chunk) == ((ci % n2) // chunk)
    tr, tc = ri % chunk, ci % chunk
    qi = lax.broadcasted_iota(jnp.int32, (n2, n2), 0)
    qj = lax.broadcasted_iota(jnp.int32, (n2, n2), 1)
    eye = jnp.where(qi == qj, 1.0, 0.0)
    level_masks = [(qi // 2) == (qj // 2)]
    size = 2
    while size < chunk:
        level_masks.append(((qi // (2 * size)) == (qj // (2 * size)))
                           & ((qi // size) != (qj // size)))
        size *= 2
    n_scr = len(scratch) // 2
    dirs = ((kd0_ref, ag0_ref, lw0_ref) + tuple(scratch[:n_scr]),
            (kd1_ref, ag1_ref, lw1_ref) + tuple(scratch[n_scr:]))

    def chunk_prefix(x):
        pos = lax.broadcasted_iota(jnp.int32, x.shape, 0) % chunk
        shift = 1
        while shift < chunk:
            x = x + jnp.where(pos >= shift, pltpu.roll(x, shift, 0), 0.0)
            shift *= 2
        return x

    masks = ((same_head & (tr > tc), same_head & (tr >= tc)),
             (same_head & (tr < tc), same_head & (tr <= tc)))

    grp = WKV_GROUP
    grows = grp * chunk

    def bmm(a, b):
        return jnp.einsum('gmk,gkn->gmn', a, b, preferred_element_type=F32)


    def phase1(g, carry):
        rows = pl.ds(pl.multiple_of(g * grows, grows), grows)
        slab = pl.ds(pl.multiple_of(g * grp, grp), grp)
        pairs = pl.ds(pl.multiple_of(g * (grp // 2), grp // 2), grp // 2)

        def load(ref):
            return ref[rows, :].reshape(grp, chunk, LANES)

        r = load(r_ref).astype(F32)
        kk = load(kk_ref).astype(F32)
        v_s = _stack_heads(load(v_ref))
        v_pad = jnp.concatenate([jnp.zeros((grp, n2, LANES), BF16), v_s], axis=2)
        y_loc = []
        for d, (kd_ref, ag_ref, lw_ref, rt_ref, mc_ref, dl_ref, dec_ref,
                mc2_ref, dl2_ref, dec2_ref) in enumerate(dirs):
            strict, incl = masks[d]
            lw = lw_ref[rows, :]
            pre = chunk_prefix(lw).reshape(grp, chunk, LANES)
            lw = lw.reshape(grp, chunk, LANES)
            tot = pre[:, chunk - 1:chunk, :]
            if d == 0:
                c_in = pre
                c_ex = pre - lw
            else:
                c_ex = tot - pre
                c_in = c_ex + lw
            kd = load(kd_ref).astype(F32)
            b = kk * load(ag_ref).astype(F32)
            e_neg = jnp.exp(-c_in)
            e_end = jnp.exp(tot - c_in)
            a_t = _stack_heads(-kk * jnp.exp(c_ex))
            r_t = _stack_heads(r * jnp.exp(c_in))
            b_t = (b * e_neg).astype(BF16)
            k_t = (kd * e_neg).astype(BF16)
            lhs = jnp.concatenate([a_t, r_t], axis=1).astype(BF16)
            rhs = jnp.concatenate([b_t, b_t, k_t, k_t], axis=1)
            gram = jnp.einsum('gmd,gnd->gmn', lhs, rhs, preferred_element_type=F32)
            a_abk = jnp.where(strict, gram[:, :n2, :], 0.0)
            a_ab = a_abk[:, :, :n2]
            a_ak = a_abk[:, :, n2:].astype(BF16)
            m_rbk = jnp.where(incl, gram[:, n2:, :], 0.0).astype(BF16)
            inv = eye + jnp.where(level_masks[0], a_ab, 0.0)
            for lvl in range(1, len(level_masks)):
                off = jnp.where(level_masks[lvl], a_ab, 0.0).astype(BF16)
                inv_bf = inv.astype(BF16)
                inv = inv + bmm(bmm(inv_bf, off).astype(BF16), inv_bf)
            x0 = jnp.concatenate([a_t.astype(BF16), bmm(a_ak, v_s).astype(BF16)], axis=2)
            wu = bmm(inv.astype(BF16), x0)
            q = jnp.concatenate([wu.astype(BF16), v_pad], axis=1)
            ry = bmm(m_rbk, q)
            rt_ref[slab] = (r_t + ry[:, :, :LANES]).astype(BF16)
            y_loc.append(ry[:, :chunk, LANES:] + ry[:, chunk:, LANES:])
            bk = jnp.concatenate([_stack_heads(b * e_end), _stack_heads(kd * e_end)],
                                 axis=1).astype(BF16)
            mc = jnp.einsum('gkm,gkn->gmn', q[:, :n2, :LANES], bk[:, :n2, :],
                            preferred_element_type=F32)
            dl = jnp.einsum('gkm,gkn->gmn', q[:, :, LANES:], bk, preferred_element_type=F32)
            dec = jnp.exp(tot)
            mc_ref[slab] = mc.astype(BF16)
            dl_ref[slab] = dl
            dec_ref[slab] = jnp.broadcast_to(dec, (grp, 8, LANES))
            first, second = (0, 1) if d == 0 else (1, 0)

            def pick(a, which):
                return a.reshape((grp // 2, 2) + a.shape[1:])[:, which]

            dec_a, dec_b = pick(dec, first), pick(dec, second)
            f1 = pick(mc, first) + eye * dec_a
            fd = jnp.concatenate([f1, pick(dl, first)], axis=1).astype(BF16)
            prod = bmm(fd, pick(mc, second).astype(BF16))
            mc2_ref[pairs] = (prod[:, :LANES, :] + pick(mc, first) * dec_b).astype(BF16)
            dl2_ref[pairs] = prod[:, LANES:, :] + pick(dl, first) * dec_b + pick(dl, second)
            dec2_ref[pairs] = jnp.broadcast_to(dec_a * dec_b, (grp // 2, 8, LANES))
        y_ref[rows, :] = (y_loc[0] + y_loc[1]).reshape(grows, LANES)
        return carry

    lax.fori_loop(0, n_chunks // grp, phase1, 0)

    n_pairs = n_chunks // 2

    def add_y(rt_ref, c, s_bf):
        rows = pl.ds(pl.multiple_of(c * chunk, chunk), chunk)
        y = _dot_nt(rt_ref[c], s_bf)
        y_ref[rows, :] = y_ref[rows, :] + (y[:chunk] + y[chunk:])

    def second_chunk(d, pair):
        return 2 * pair + 1 if d == 0 else 2 * pair

    def phase2(step, carry):
        new = []
        for d, (*_ins, rt_ref, mc_ref, dl_ref, dec_ref, mc2_ref, dl2_ref, dec2_ref) in enumerate(dirs):
            pair = step if d == 0 else n_pairs - 1 - step
            prev = jnp.maximum(step - 1, 0) if d == 0 else jnp.minimum(n_pairs - step, n_pairs - 1)
            c1 = 2 * pair if d == 0 else 2 * pair + 1
            s0, mid_prev = carry[d]
            s0_bf = s0.astype(BF16)
            s_next = s0 * dec2_ref[pair][0:1, :] + _dot(s0_bf, mc2_ref[pair]) + dl2_ref[pair]
            mid = s0 * dec_ref[c1][0:1, :] + _dot(s0_bf, mc_ref[c1]) + dl_ref[c1]
            add_y(rt_ref, c1, s0_bf)
            add_y(rt_ref, second_chunk(d, prev), mid_prev)
            new.append((s_next, mid.astype(BF16)))
        return tuple(new)

    zero = jnp.zeros((LANES, LANES), F32)
    init = (zero, zero.astype(BF16))
    final = lax.fori_loop(0, n_pairs, phase2, (init, init), unroll=8)
    for d, (*_ins, rt_ref, _mc, _dl, _dec, _mc2, _dl2, _dec2) in enumerate(dirs):
        last_pair = n_pairs - 1 if d == 0 else 0
        add_y(rt_ref, second_chunk(d, last_pair), final[d][1])


def _wkv(r, v, kk, kd0, kd1, ag0, ag1, lw0, lw1, b):
    spec = pl.BlockSpec((None, SEQ, LANES), lambda bi, hp: (bi, 0, hp))
    args = [a.reshape(b, SEQ, C) for a in (r, v, kk, kd0, kd1, ag0, ag1, lw0, lw1)]
    n_chunks = SEQ // WKV_CHUNK
    kern = functools.partial(_wkv_kernel, chunk=WKV_CHUNK, n_chunks=n_chunks)
    per_dir = [pltpu.VMEM((n_chunks, LANES, LANES), BF16),
               pltpu.VMEM((n_chunks, LANES, LANES), BF16),
               pltpu.VMEM((n_chunks, LANES, LANES), F32),
               pltpu.VMEM((n_chunks, 8, LANES), F32),
               pltpu.VMEM((n_chunks // 2, LANES, LANES), BF16),
               pltpu.VMEM((n_chunks // 2, LANES, LANES), F32),
               pltpu.VMEM((n_chunks // 2, 8, LANES), F32)]
    scratch = per_dir + per_dir
    return pl.pallas_call(
        kern,
        grid=(b, C // LANES),
        in_specs=[spec] * 9,
        out_specs=spec,
        out_shape=jax.ShapeDtypeStruct((b, SEQ, C), F32),
        scratch_shapes=scratch,
        compiler_params=_params("parallel", "parallel"),
        name="wkv7",
    )(*args)


def _rwkv_post_kernel(y_ref, r_ref, kd0_ref, kd1_ref, v_ref, g_ref, x_ref, lng_ref, lnb_ref,
                      rk_ref, wo_ref, o_ref):
    ones = _head_ones(COL_CHUNK)
    y = y_ref[...]
    dev = y - _head_sum(y, ones) * (1.0 / HD)
    var = _head_sum(dev * dev, ones) * (1.0 / HD)
    yn = dev * lax.rsqrt(var + RW_GN_EPS) * lng_ref[...] + lnb_ref[...]
    k_bonus = 0.5 * (kd0_ref[...].astype(F32) + kd1_ref[...].astype(F32))
    bonus = _head_sum(r_ref[...].astype(F32) * k_bonus * rk_ref[...], ones) * v_ref[...].astype(F32)
    out = ((yn + bonus) * g_ref[...].astype(F32)).astype(BF16)
    o_ref[...] = x_ref[...] + _dot(out, wo_ref[...])


def _rwkv_post(y, r, kd0, kd1, v, g, x, ln_g, ln_b, r_k, w_o, tm=512):
    t = x.shape[0]
    row = pl.BlockSpec((tm, C), lambda i: (i, 0))
    vec = _const_spec((1, C))
    return pl.pallas_call(
        _rwkv_post_kernel,
        grid=(t // tm,),
        in_specs=[row] * 7 + [vec, vec, vec, _const_spec((C, C))],
        out_specs=row,
        out_shape=jax.ShapeDtypeStruct((t, C), F32),
        compiler_params=_params("parallel"),
        name="rwkv_post",
    )(y, r, kd0, kd1, v, g, x, ln_g.reshape(1, C), ln_b.reshape(1, C), r_k.reshape(1, C),
      w_o.astype(BF16))


def _lambda_init(layer_idx):
    return 0.8 - 0.6 * math.exp(-0.3 * layer_idx)


def _head_gain(q_gain, n_q, k_gain, n_k, n_v):
    scale = HD ** -0.5 * LOG2E
    return jnp.concatenate([jnp.tile(q_gain * scale, n_q // HD), jnp.tile(k_gain, n_k // HD),
                            jnp.ones((n_v,), F32)])


def _encode(x, b, p):
    t = b * SEQ
    x = x.reshape(t, C)
    pos = jnp.arange(SEQ, dtype=F32)

    hn = _head_gain(p['a_q_norm'], C, p['a_k_norm'], C, C)
    qkv = _proj(x, p['n0_attn'], p['a_w_qkv'].astype(BF16), hn, 2 * C,
                rope=_rope_tables(pos, None, HD // 2), rope_half=HD // 2)
    lam_init = _lambda_init(0)
    lam = (jnp.exp(jnp.sum(p['a_lq1'] * p['a_lk1'])) - jnp.exp(jnp.sum(p['a_lq2'] * p['a_lk2']))
           + lam_init).reshape(1)
    o = _diff_attn(qkv.reshape(b, SEQ, 3 * C), lam,
                   (p['a_subln'] * (1.0 - lam_init)).reshape(1, LANES), b)
    x = _out_proj(o.reshape(t, C), p['a_w_o'].astype(BF16), x)
    x = _ffn(x, p['n0_ffn'], p['f0_w_in'], p['f0_conv_w'], p['f0_conv_b'], p['f0_w_out'])

    kvw = GQ_KV * HD
    w = p['b_w_qkv']

    def dup(wc):
        wc = wc.reshape(C, GQ_KV, 1, HD)
        return jnp.broadcast_to(wc, (C, GQ_KV, 2, HD)).reshape(C, 2 * kvw)

    w_dup = jnp.concatenate([w[:, :C], dup(w[:, C:C + kvw]), dup(w[:, C + kvw:])], axis=1)
    hn = _head_gain(p['b_q_norm'], C, p['b_k_norm'], 2 * kvw, 2 * kvw)
    tok = jnp.arange(SEQ)
    rope = _rope_tables((tok // GRID_W).astype(F32), (tok % GRID_W).astype(F32), HD // 4)
    qkv = _proj(x, p['n1_attn'], w_dup.astype(BF16), hn, C + 2 * kvw, rope=rope,
                rope_half=HD // 4)
    o = _gqa(qkv.reshape(b, SEQ, C + 4 * kvw), b)
    x = _out_proj(o.reshape(t, C), p['b_w_o'].astype(BF16), x)
    x = _ffn(x, p['n1_ffn'], p['f1_w_in'], p['f1_conv_w'], p['f1_conv_b'], p['f1_w_out'])

    hn = _head_gain(p['c_q_norm'], C, p['c_k_norm'], C, C)
    qkv = _proj(x, p['n2_attn'], p['c_w_qkv'].astype(BF16), hn, 2 * C)
    o = _na_attn(qkv.reshape(b, SEQ, 3 * C), _na_bias_table(p['c_rel_bias']), b)
    x = _out_proj(o.reshape(t, C), p['c_w_o'].astype(BF16), x)
    x = _ffn(x, p['n2_ffn'], p['f2_w_in'], p['f2_conv_w'], p['f2_conv_b'], p['f2_w_out'])

    r, v, g, kk, kd0, kd1, ag0, ag1, lw0, lw1 = _rwkv_pre(
        x, p['n3_attn'], p['d_mu'], p['d_w_r'], p['d_w_k'], p['d_w_v'], p['d_g1'], p['d_g2'],
        p['d_k_k'], p['d_k_a'], p['d_w0'], p['d_w1'], p['d_w2'], p['d_a0'], p['d_a1'], p['d_a2'])
    y = _wkv(r, v, kk, kd0, kd1, ag0, ag1, lw0, lw1, b)
    x = _rwkv_post(y.reshape(t, C), r, kd0, kd1, v, g, x, p['d_ln_g'], p['d_ln_b'], p['d_r_k'],
                   p['d_w_o'])
    x = _ffn(x, p['n3_ffn'], p['f3_w_in'], p['f3_conv_w'], p['f3_conv_b'], p['f3_w_out'])
    return x.reshape(b, SEQ, C)


def kernel(x_prompt, x_sample,
           n0_attn, n0_ffn, n1_attn, n1_ffn, n2_attn, n2_ffn, n3_attn, n3_ffn, a_w_qkv,
           a_q_norm, a_k_norm, a_lq1, a_lk1, a_lq2, a_lk2, a_subln, a_w_o, b_w_qkv,
           b_q_norm, b_k_norm, b_w_o, c_w_qkv, c_q_norm, c_k_norm, c_rel_bias, c_w_o, d_mu,
           d_w_r, d_w_k, d_w_v, d_w_o, d_g1, d_g2, d_k_k, d_k_a, d_r_k, d_ln_g, d_ln_b,
           d_w0, d_w1, d_w2, d_a0, d_a1, d_a2, f0_w_in, f0_conv_w, f0_conv_b, f0_w_out,
           f1_w_in, f1_conv_w, f1_conv_b, f1_w_out, f2_w_in, f2_conv_w, f2_conv_b, f2_w_out,
           f3_w_in, f3_conv_w, f3_conv_b, f3_w_out):
    p = dict(locals())
    xs = (p.pop('x_prompt'), p.pop('x_sample'))
    return tuple(_encode(x, x.shape[0], p) for x in xs)
```

```python
import functools
import math

import numpy as np
import jax
import jax.numpy as jnp
from jax import lax
from jax.experimental import pallas as pl
from jax.experimental.pallas import tpu as pltpu

F32 = jnp.float32
BF16 = jnp.bfloat16

C = 1024
HD = 64
LANES = 128
SEQ = 2048
GRID_W = 64
ROPE_THETA = 10000.0
NORM_EPS = 1e-6
DA_HEADS = 8
DA_SUBLN_EPS = 1e-5
GQ_KV = 4
NA_WIN_ROWS = 8
NA_WIN_COLS = 16
NA_QROWS = 4
NA_KROWS = 12
RW_GN_EPS = 64e-5
FFN_HIDDEN = 2816
WKV_CHUNK = 64
LOG2E = math.log2(math.e)

COL_CHUNK = 256
VMEM_LIMIT = 56 * 1024 * 1024


def _params(*sem):
    return pltpu.CompilerParams(dimension_semantics=sem, vmem_limit_bytes=VMEM_LIMIT)


def _rms(x, g, eps):
    return x * lax.rsqrt(jnp.mean(x * x, axis=-1, keepdims=True) + eps) * g


def _split(x):
    hi = x.astype(BF16)
    lo = (x - hi.astype(F32)).astype(BF16)
    return hi, lo


def _head_ones(n):
    r = lax.broadcasted_iota(jnp.int32, (n, n), 0)
    c = lax.broadcasted_iota(jnp.int32, (n, n), 1)
    return jnp.where((r // HD) == (c // HD), 1.0, 0.0).astype(BF16)


def _head_sum(x, ones):
    w = ones.shape[0]
    out = []
    for c in range(x.shape[1] // w):
        hi, lo = _split(x[:, c * w:(c + 1) * w])
        out.append(_dot(hi, ones) + _dot(lo, ones))
    return out[0] if len(out) == 1 else jnp.concatenate(out, axis=1)


def _dot(a, b):
    return jnp.dot(a, b, preferred_element_type=F32)


def _dot_nt(a, b):
    return lax.dot_general(a, b, (((1,), (1,)), ((), ())), preferred_element_type=F32)


def _shift_rows(u, prev_row, next_row):
    n = u.shape[0]
    up = pltpu.roll(u, 1, 0)
    dn = pltpu.roll(u, n - 1, 0)
    row = lax.broadcasted_iota(jnp.int32, (8, u.shape[1]), 0)
    up = jnp.concatenate([jnp.where(row == 0, prev_row, up[:8]), up[8:]], axis=0)
    dn = jnp.concatenate([dn[:n - 8], jnp.where(row == 7, next_row, dn[n - 8:])], axis=0)
    return up, dn


def _lo_mask(shape):
    return lax.broadcasted_iota(jnp.int32, shape, len(shape) - 1) < HD


def _stack_heads(x):
    m = _lo_mask(x.shape)
    zero = jnp.zeros_like(x)
    return jnp.concatenate([jnp.where(m, x, zero), jnp.where(m, zero, x)], axis=-2)


def _proj_kernel(*refs, n_chunks, n_norm_chunks, rope_half):
    if rope_half:
        x_ref, g_ref, w_ref, hn_ref, cos_ref, sin_ref, o_ref = refs
    else:
        x_ref, g_ref, w_ref, hn_ref, o_ref = refs
    xn = _rms(x_ref[...], g_ref[...], NORM_EPS).astype(BF16)
    ones = _head_ones(COL_CHUNK)
    if rope_half:
        lane = lax.broadcasted_iota(jnp.int32, (1, COL_CHUNK), 1)
        first = (lane % (2 * rope_half)) < rope_half
    def mm(c):
        return _dot(xn, w_ref[:, c * COL_CHUNK:(c + 1) * COL_CHUNK])

    y_next = mm(0)
    for c in range(n_chunks):
        cols = slice(c * COL_CHUNK, (c + 1) * COL_CHUNK)
        y = y_next
        if c + 1 < n_chunks:
            y_next = mm(c + 1)
        if c < n_norm_chunks:
            ms = _dot((y * y).astype(BF16), ones) * (1.0 / HD)
            y = y * lax.rsqrt(ms + NORM_EPS) * hn_ref[:, cols]
            if rope_half:
                rot = jnp.where(first, pltpu.roll(y, COL_CHUNK - rope_half, 1),
                                pltpu.roll(y, rope_half, 1))
                y = y * cos_ref[...] + rot * sin_ref[...]
        o_ref[:, cols] = y.astype(BF16)


def _proj(x, g, w, hn, n_norm_cols, rope=None, rope_half=0, tm=512):
    t = x.shape[0]
    n = w.shape[1]
    tiles_per_seq = SEQ // tm
    in_specs = [
        pl.BlockSpec((tm, C), lambda i: (i, 0)),
        pl.BlockSpec((1, C), lambda i: (0, 0)),
        pl.BlockSpec((C, n), lambda i: (0, 0)),
        pl.BlockSpec((1, n), lambda i: (0, 0)),
    ]
    args = [x, g.reshape(1, C), w, hn.reshape(1, n)]
    if rope_half:
        spec = pl.BlockSpec((tm, COL_CHUNK), lambda i: (i % tiles_per_seq, 0))
        in_specs += [spec, spec]
        args += [rope[0], rope[1]]
    kern = functools.partial(_proj_kernel, n_chunks=n // COL_CHUNK,
                             n_norm_chunks=n_norm_cols // COL_CHUNK, rope_half=rope_half)
    return pl.pallas_call(
        kern,
        grid=(t // tm,),
        in_specs=in_specs,
        out_specs=pl.BlockSpec((tm, n), lambda i: (i, 0)),
        out_shape=jax.ShapeDtypeStruct((t, n), BF16),
        compiler_params=_params("parallel"),
        name="proj",
    )(*args)


def _rope_tables(pos_first, pos_second, half):
    inv = ROPE_THETA ** (-jnp.arange(half, dtype=F32) / half)

    def one(pos):
        ang = pos[:, None] * inv[None, :]
        cos = jnp.concatenate([jnp.cos(ang), jnp.cos(ang)], axis=-1)
        sin = jnp.concatenate([-jnp.sin(ang), jnp.sin(ang)], axis=-1)
        return cos, sin

    c1, s1 = one(pos_first)
    if pos_second is None:
        cos, sin = c1, s1
    else:
        c2, s2 = one(pos_second)
        cos = jnp.concatenate([c1, c2], axis=-1)
        sin = jnp.concatenate([s1, s2], axis=-1)
    reps = COL_CHUNK // cos.shape[-1]
    return jnp.tile(cos, (1, reps)), jnp.tile(sin, (1, reps))


def _out_kernel(a_ref, w_ref, x_ref, o_ref):
    o_ref[...] = x_ref[...] + _dot(a_ref[...], w_ref[...])


def _out_proj(a, w, x, tm=1024):
    t = x.shape[0]
    return pl.pallas_call(
        _out_kernel,
        grid=(t // tm,),
        in_specs=[pl.BlockSpec((tm, C), lambda i: (i, 0)),
                  pl.BlockSpec((C, C), lambda i: (0, 0)),
                  pl.BlockSpec((tm, C), lambda i: (i, 0))],
        out_specs=pl.BlockSpec((tm, C), lambda i: (i, 0)),
        out_shape=jax.ShapeDtypeStruct((t, C), F32),
        compiler_params=_params("parallel"),
        name="out_proj",
    )(a, w, x)


HALO = 8


def _halo_rows(xp_ref, xn_ref, tiles_per_seq):
    i = pl.program_id(0)
    pos = i % tiles_per_seq
    halo = jnp.concatenate([xp_ref[...], xn_ref[...]], axis=0)
    row = lax.broadcasted_iota(jnp.int32, halo.shape, 0)
    has_prev = (pos != 0).astype(jnp.int32)
    has_next = (pos != tiles_per_seq - 1).astype(jnp.int32)
    keep = jnp.where(row < HALO, has_prev, has_next)
    return jnp.where(keep != 0, halo, 0.0)


def _ffn_kernel(x_ref, xp_ref, xn_ref, g_ref, wi_ref, cw_ref, cb_ref, wo_ref, o_ref, hid_ref, *,
                tm, tiles_per_seq):
    x = x_ref[...]
    xe = jnp.concatenate([x, _halo_rows(xp_ref, xn_ref, tiles_per_seq)], axis=0)
    xne = _rms(xe, g_ref[...], NORM_EPS).astype(BF16)

    def in_proj(c):
        gate_cols = slice(c * COL_CHUNK, (c + 1) * COL_CHUNK)
        val_cols = slice(FFN_HIDDEN + c * COL_CHUNK, FFN_HIDDEN + (c + 1) * COL_CHUNK)
        return _dot(xne, wi_ref[:, gate_cols]), _dot(xne, wi_ref[:, val_cols])

    def conv(u, cols):
        um = u[:tm]
        up, dn = _shift_rows(um, u[tm + HALO - 1:tm + HALO], u[tm + HALO:tm + HALO + 1])
        cw = cw_ref[:, cols]
        return up * cw[0:1] + um * cw[1:2] + dn * cw[2:3] + cb_ref[:, cols]

    n_chunks = FFN_HIDDEN // COL_CHUNK
    u_next = in_proj(0)
    for c in range(n_chunks):
        ug, uv = u_next
        if c + 1 < n_chunks:
            u_next = in_proj(c + 1)
        cols = slice(c * COL_CHUNK, (c + 1) * COL_CHUNK)
        gate = conv(ug, cols)
        val = conv(uv, slice(FFN_HIDDEN + c * COL_CHUNK, FFN_HIDDEN + (c + 1) * COL_CHUNK))
        hid_ref[:, cols] = (gate * jax.nn.sigmoid(gate) * val).astype(BF16)
    o_ref[...] = x + _dot(hid_ref[...], wo_ref[...])


def _halo_specs(tm, t):
    per = tm // HALO
    last = t // HALO - 1
    return [pl.BlockSpec((HALO, C), lambda i: (jnp.maximum(i * per - 1, 0), 0)),
            pl.BlockSpec((HALO, C), lambda i: (jnp.minimum((i + 1) * per, last), 0))]


def _const_spec(shape):
    return pl.BlockSpec(shape, lambda i: (0,) * len(shape), pipeline_mode=pl.Buffered(1))


def _ffn(x, g, w_in, conv_w, conv_b, w_out, tm=512):
    t = x.shape[0]
    kern = functools.partial(_ffn_kernel, tm=tm, tiles_per_seq=SEQ // tm)
    return pl.pallas_call(
        kern,
        grid=(t // tm,),
        in_specs=[pl.BlockSpec((tm, C), lambda i: (i, 0))] + _halo_specs(tm, t) + [
            _const_spec((1, C)),
            _const_spec((C, 2 * FFN_HIDDEN)),
            _const_spec((3, 2 * FFN_HIDDEN)),
            _const_spec((1, 2 * FFN_HIDDEN)),
            _const_spec((FFN_HIDDEN, C))],
        out_specs=pl.BlockSpec((tm, C), lambda i: (i, 0)),
        out_shape=jax.ShapeDtypeStruct((t, C), F32),
        scratch_shapes=[pltpu.VMEM((tm, FFN_HIDDEN), BF16)],
        compiler_params=_params("parallel"),
        name="conv_ffn",
    )(x, x, x, g.reshape(1, C), w_in.astype(BF16), conv_w, conv_b.reshape(1, -1),
      w_out.astype(BF16))


ATTN_SUB = 256


def _softmax_rows(s):
    p = jnp.exp2(s - jnp.max(s, axis=-1, keepdims=True))
    return p, 1.0 / jnp.sum(p, axis=-1, keepdims=True)


def _diff_attn_kernel(lam_ref, q_ref, k_ref, v_ref, sg_ref, o_ref, *, tq):
    k = k_ref[...]
    v = v_ref[...]
    n = ATTN_SUB // 2
    n_tiles = tq // n

    def scores(t):
        return _dot_nt(_stack_heads(q_ref[t * n:(t + 1) * n, :]), k)

    s_next = scores(0)
    for t in range(n_tiles):
        s = s_next
        if t + 1 < n_tiles:
            s_next = scores(t + 1)
        p, inv = _softmax_rows(s)
        o2 = _dot(p.astype(BF16), v) * inv
        o = o2[:n] - lam_ref[0] * o2[n:]
        o_ref[t * n:(t + 1) * n, :] = _rms(o, sg_ref[...], DA_SUBLN_EPS).astype(BF16)


def _diff_attn(qkv, lam, subln_gain, b, tq=2048):
    q_blocks = C // LANES
    kern = functools.partial(_diff_attn_kernel, tq=tq)
    return pl.pallas_call(
        kern,
        grid=(b, DA_HEADS, SEQ // tq),
        in_specs=[pl.BlockSpec(memory_space=pltpu.SMEM),
                  pl.BlockSpec((None, tq, LANES), lambda bi, h, qi: (bi, qi, h)),
                  pl.BlockSpec((None, SEQ, LANES), lambda bi, h, qi: (bi, 0, q_blocks + h)),
                  pl.BlockSpec((None, SEQ, LANES), lambda bi, h, qi: (bi, 0, 2 * q_blocks + h)),
                  pl.BlockSpec((1, LANES), lambda bi, h, qi: (0, 0))],
        out_specs=pl.BlockSpec((None, tq, LANES), lambda bi, h, qi: (bi, qi, h)),
        out_shape=jax.ShapeDtypeStruct((b, SEQ, C), BF16),
        compiler_params=_params("parallel", "parallel", "parallel"),
        name="diff_attn",
    )(lam, qkv, qkv, qkv, subln_gain)


def _gqa_kernel(q_ref, k_ref, v_ref, o_ref, *, tq):
    k = k_ref[...]
    v = v_ref[...]
    n = ATTN_SUB // 2
    m = _lo_mask((n, LANES))
    tiles = [(slice(t * n, (t + 1) * n), slice(half * LANES, (half + 1) * LANES))
             for t in range(tq // n) for half in range(2)]

    def scores(tile):
        return _dot_nt(_stack_heads(q_ref[tile]), k)

    s_next = scores(tiles[0])
    for i, tile in enumerate(tiles):
        s = s_next
        if i + 1 < len(tiles):
            s_next = scores(tiles[i + 1])
        p, inv = _softmax_rows(s)
        o = _dot(p.astype(BF16), v) * inv
        o_ref[tile] = jnp.where(m, o[:n], o[n:]).astype(BF16)


def _gqa(qkv, b, tq=1024):
    kern = functools.partial(_gqa_kernel, tq=tq)
    k0 = C // LANES
    return pl.pallas_call(
        kern,
        grid=(b, GQ_KV, SEQ // tq),
        in_specs=[pl.BlockSpec((None, tq, 2 * LANES), lambda bi, j, qi: (bi, qi, j)),
                  pl.BlockSpec((None, SEQ, LANES), lambda bi, j, qi: (bi, 0, k0 + j)),
                  pl.BlockSpec((None, SEQ, LANES), lambda bi, j, qi: (bi, 0, k0 + GQ_KV + j))],
        out_specs=pl.BlockSpec((None, tq, 2 * LANES), lambda bi, j, qi: (bi, qi, j)),
        out_shape=jax.ShapeDtypeStruct((b, SEQ, C), BF16),
        compiler_params=_params("parallel", "parallel", "parallel"),
        name="gqa_attn",
    )(qkv, qkv, qkv)


NA_TQ = NA_QROWS * GRID_W
NA_TK = NA_KROWS * GRID_W
NA_KBLK = NA_TK // NA_TQ
NA_STEPS = SEQ // NA_TQ
NA_SUB = 512
NA_BATCH = 8


def _na_kernel(q_ref, k0_ref, k1_ref, k2_ref, v0_ref, v1_ref, v2_ref, bias_ref, o_ref, *, nb):
    n = NA_SUB // 2
    m = _lo_mask((n, LANES))
    tiles = [(bi, t) for bi in range(nb) for t in range(NA_TQ // n)]

    def scores(tile):
        bi, t = tile
        k = jnp.concatenate([k0_ref[bi], k1_ref[bi], k2_ref[bi]], axis=0)
        return _dot_nt(_stack_heads(q_ref[bi, t * n:(t + 1) * n, :]), k)

    s_next = scores(tiles[0])
    for i, (bi, t) in enumerate(tiles):
        s = s_next
        if i + 1 < len(tiles):
            s_next = scores(tiles[i + 1])
        rows = slice(t * n, (t + 1) * n)
        bias = jnp.concatenate([bias_ref[rows, :],
                                bias_ref[NA_TQ + t * n:NA_TQ + (t + 1) * n, :]], axis=0)
        p, inv = _softmax_rows(s + bias)
        v = jnp.concatenate([v0_ref[bi], v1_ref[bi], v2_ref[bi]], axis=0)
        o = _dot(p.astype(BF16), v) * inv
        o_ref[bi, rows, :] = jnp.where(m, o[:n], o[n:]).astype(BF16)


def _na_key_block(i):
    return jnp.clip(i - 1, 0, NA_STEPS - NA_KBLK)


def _na_pattern(i):
    return jnp.minimum(i, 2) + i // (NA_STEPS - 1)


def _na_attn(qkv, bias, b):
    nblk = C // LANES
    nb = math.gcd(b, NA_BATCH)

    def kv_spec(base, j):
        return pl.BlockSpec((nb, NA_TQ, LANES),
                            lambda i, hp, bi: (bi, _na_key_block(i) + j, base + hp))

    return pl.pallas_call(
        functools.partial(_na_kernel, nb=nb),
        grid=(NA_STEPS, nblk, b // nb),
        in_specs=[pl.BlockSpec((nb, NA_TQ, LANES), lambda i, hp, bi: (bi, i, hp))]
        + [kv_spec(nblk, j) for j in range(NA_KBLK)]
        + [kv_spec(2 * nblk, j) for j in range(NA_KBLK)]
        + [pl.BlockSpec((None, None, 2 * NA_TQ, NA_TK),
                        lambda i, hp, bi: (_na_pattern(i), hp, 0, 0))],
        out_specs=pl.BlockSpec((nb, NA_TQ, LANES), lambda i, hp, bi: (bi, i, hp)),
        out_shape=jax.ShapeDtypeStruct((b, SEQ, C), BF16),
        compiler_params=_params("parallel", "parallel", "parallel"),
        name="na_attn",
    )(qkv, qkv, qkv, qkv, qkv, qkv, qkv, bias)


def _na_bias_table(rel_bias):
    rows = SEQ // GRID_W
    wr, wc = NA_WIN_ROWS, NA_WIN_COLS
    heads = rel_bias.shape[0]
    cols = np.arange(GRID_W)
    col_start = np.clip(cols - wc // 2, 0, GRID_W - wc)
    col_mask = (cols[None, :] >= col_start[:, None]) & (cols[None, :] < col_start[:, None] + wc)
    dc = np.clip(cols[None, :] - cols[:, None] + (wc - 1), 0, 2 * wc - 2)
    onehot = (dc.reshape(-1)[None, :] == np.arange(2 * wc - 1)[:, None]).astype(np.float32)
    tiles = jnp.dot(rel_bias.reshape(heads * (2 * wr - 1), 2 * wc - 1), jnp.asarray(onehot),
                    precision=lax.Precision.HIGHEST)
    tiles = tiles.reshape(heads, 2 * wr - 1, GRID_W, GRID_W)
    tiles = jnp.where(jnp.asarray(col_mask)[None, None], tiles * LOG2E, -1e30)
    tiles = jnp.concatenate([tiles, jnp.full((heads, 1, GRID_W, GRID_W), -1e30, F32)], axis=1)
    n_dr = 2 * wr - 1
    idx = np.full((4, NA_QROWS, NA_KROWS), n_dr, np.int32)
    starts = [(0, 0), (NA_QROWS, 0), (2 * NA_QROWS, NA_QROWS), (rows - NA_QROWS, rows - NA_KROWS)]
    for p, (r0, ks) in enumerate(starts):
        for a in range(NA_QROWS):
            r = r0 + a
            rs = min(max(r - wr // 2, 0), rows - wr)
            for bb in range(NA_KROWS):
                j = ks + bb
                if rs <= j < rs + wr:
                    idx[p, a, bb] = j - r + (wr - 1)
    big = tiles[:, jnp.asarray(idx)]
    big = big.transpose(1, 0, 2, 4, 3, 5)
    return big.reshape(4, heads // 2, 2 * NA_TQ, NA_TK)


def _rwkv_pre_kernel(x_ref, xp_ref, xn_ref, g_ref, mu_ref, wr_ref, wk_ref, wv_ref, g1_ref, g2_ref,
                     w1_ref, w2_ref, a1_ref, a2_ref, w0_ref, a0_ref, kk_ref, ka_ref,
                     r_o, v_o, g_o, kk_o, kd0_o, kd1_o, ag0_o, ag1_o, lw0_o, lw1_o, *,
                     tiles_per_seq):
    g = g_ref[...]
    h = _rms(x_ref[...], g, NORM_EPS)
    hh = _rms(_halo_rows(xp_ref, xn_ref, tiles_per_seq), g, NORM_EPS)
    up, dn = _shift_rows(h, hh[HALO - 1:HALO], hh[HALO:HALO + 1])
    xx = 0.5 * (up + dn) - h

    def mix(j):
        return (h + xx * mu_ref[j:j + 1, :]).astype(BF16)

    tw = jnp.tanh(_dot(mix(1), w1_ref[...]))
    ta = _dot(mix(4), a1_ref[...])
    gate = jax.nn.sigmoid(_dot(mix(5), g1_ref[...]))
    lo = _lo_mask(tw.shape)
    sels = (lo, jnp.logical_not(lo))
    zs = [_dot(jnp.where(s, tw, 0.0).astype(BF16), w2_ref[...]) for s in sels]
    az = [_dot(jnp.where(s, ta, 0.0).astype(BF16), a2_ref[...]) for s in sels]
    k = _dot(mix(2), wk_ref[...])
    r_o[...] = _dot(mix(0), wr_ref[...]).astype(BF16)
    v_o[...] = _dot(mix(3), wv_ref[...]).astype(BF16)
    g_o[...] = _dot(gate.astype(BF16), g2_ref[...]).astype(BF16)

    kkv = k * kk_ref[...]
    sq = _head_sum(kkv * kkv, _head_ones(COL_CHUNK))
    kk_o[...] = (kkv * lax.rsqrt(jnp.maximum(sq, 1e-24))).astype(BF16)
    ka = ka_ref[...]
    for d, (kd_o, ag_o, lw_o) in enumerate(((kd0_o, ag0_o, lw0_o), (kd1_o, ag1_o, lw1_o))):
        lw_o[...] = -math.exp(-0.5) * jax.nn.sigmoid(w0_ref[d:d + 1, :] + zs[d])
        a = jax.nn.sigmoid(a0_ref[d:d + 1, :] + az[d])
        ag_o[...] = a.astype(BF16)
        kd_o[...] = (k * (1.0 + (a - 1.0) * ka)).astype(BF16)


def _rwkv_pre(x, g, mu, w_r, w_k, w_v, g1, g2, k_k, k_a, w0, w1, w2, a0, a1, a2, tm=512):
    t = x.shape[0]
    lora = w1.shape[-1]
    w1c = jnp.concatenate([w1[0], w1[1]], axis=1).astype(BF16)
    a1c = jnp.concatenate([a1[0], a1[1]], axis=1).astype(BF16)
    w2c = w2.reshape(2 * lora, C).astype(BF16)
    a2c = a2.reshape(2 * lora, C).astype(BF16)
    assert 2 * lora == LANES
    glora = g1.shape[-1]
    row = pl.BlockSpec((tm, C), lambda i: (i, 0))
    kern = functools.partial(_rwkv_pre_kernel, tiles_per_seq=SEQ // tm)
    bf = jax.ShapeDtypeStruct((t, C), BF16)
    f32 = jax.ShapeDtypeStruct((t, C), F32)
    return pl.pallas_call(
        kern,
        grid=(t // tm,),
        in_specs=[row] + _halo_specs(tm, t) + [
            _const_spec((1, C)), _const_spec((6, C)),
            _const_spec((C, C)), _const_spec((C, C)), _const_spec((C, C)),
            _const_spec((C, glora)), _const_spec((glora, C)),
            _const_spec((C, LANES)), _const_spec((LANES, C)),
            _const_spec((C, LANES)), _const_spec((LANES, C)),
            _const_spec((2, C)), _const_spec((2, C)), _const_spec((1, C)), _const_spec((1, C))],
        out_specs=[row] * 10,
        out_shape=[bf] * 8 + [f32] * 2,
        compiler_params=_params("parallel"),
        name="rwkv_pre",
    )(x, x, x, g.reshape(1, C), mu, w_r.astype(BF16), w_k.astype(BF16), w_v.astype(BF16),
      g1.astype(BF16), g2.astype(BF16), w1c, w2c, a1c, a2c, w0, a0,
      k_k.reshape(1, C), k_a.reshape(1, C))


WKV_GROUP = 16


def _wkv_kernel(r_ref, v_ref, kk_ref, kd0_ref, kd1_ref, ag0_ref, ag1_ref, lw0_ref, lw1_ref,
                y_ref, *scratch, chunk, n_chunks):
    n2 = 2 * chunk
    ri = lax.broadcasted_iota(jnp.int32, (n2, 2 * n2), 0)
    ci = lax.broadcasted_iota(jnp.int32, (n2, 2 * n2), 1)
    same_head = (ri // chunk) == ((ci % n2) // chunk)
    tr, tc = ri % chunk, ci % chunk
    qi = lax.broadcasted_iota(jnp.int32, (n2, n2), 0)
    qj = lax.broadcasted_iota(jnp.int32, (n2, n2), 1)
    eye = jnp.where(qi == qj, 1.0, 0.0)
    level_masks = [(qi // 2) == (qj // 2)]
    size = 2
    while size < chunk:
        level_masks.append(((qi // (2 * size)) == (qj // (2 * size)))
                           & ((qi // size) != (qj // size)))
        size *= 2
    n_scr = len(scratch) // 2
    dirs = ((kd0_ref, ag0_ref, lw0_ref) + tuple(scratch[:n_scr]),
            (kd1_ref, ag1_ref, lw1_ref) + tuple(scratch[n_scr:]))

    def chunk_prefix(x):
        pos = lax.broadcasted_iota(jnp.int32, x.shape, 0) % chunk
        shift = 1
        while shift < chunk:
            x = x + jnp.where(pos >= shift, pltpu.roll(x, shift, 0), 0.0)
            shift *= 2
        return x

    masks = ((same_head & (tr > tc), same_head & (tr >= tc)),
             (same_head & (tr < tc), same_head & (tr <= tc)))

    grp = WKV_GROUP
    grows = grp * chunk

    def bmm(a, b):
        return jnp.einsum('gmk,gkn->gmn', a, b, preferred_element_type=F32)


    def phase1(jobs):
        for d, g, assign in jobs:
            rows = slice(g * grows, (g + 1) * grows)
            slab = slice(g * grp, (g + 1) * grp)
            pairs = slice(g * (grp // 2), (g + 1) * (grp // 2))
            (kd_ref, ag_ref, lw_ref, rt_ref, mc_ref, dl_ref, dec_ref,
             mc2_ref, dl2_ref, dec2_ref) = dirs[d]

            def load(ref, rows=rows):
                return ref[rows, :].reshape(grp, chunk, LANES)

            r = load(r_ref).astype(F32)
            kk = load(kk_ref).astype(F32)
            v_s = _stack_heads(load(v_ref))
            v_pad = jnp.concatenate([jnp.zeros((grp, n2, LANES), BF16), v_s], axis=2)
            strict, incl = masks[d]
            lw = lw_ref[rows, :]
            pre = chunk_prefix(lw).reshape(grp, chunk, LANES)
            lw = lw.reshape(grp, chunk, LANES)
            tot = pre[:, chunk - 1:chunk, :]
            if d == 0:
                c_in = pre
                c_ex = pre - lw
            else:
                c_ex = tot - pre
                c_in = c_ex + lw
            kd = load(kd_ref).astype(F32)
            b = kk * load(ag_ref).astype(F32)
            e_neg = jnp.exp(-c_in)
            e_end = jnp.exp(tot - c_in)
            a_t = _stack_heads(-kk * jnp.exp(c_ex))
            r_t = _stack_heads(r * jnp.exp(c_in))
            b_t = (b * e_neg).astype(BF16)
            k_t = (kd * e_neg).astype(BF16)
            lhs = jnp.concatenate([a_t, r_t], axis=1).astype(BF16)
            rhs = jnp.concatenate([b_t, b_t, k_t, k_t], axis=1)
            gram = jnp.einsum('gmd,gnd->gmn', lhs, rhs, preferred_element_type=F32)
            a_abk = jnp.where(strict, gram[:, :n2, :], 0.0)
            a_ab = a_abk[:, :, :n2]
            a_ak = a_abk[:, :, n2:].astype(BF16)
            m_rbk = jnp.where(incl, gram[:, n2:, :], 0.0).astype(BF16)
            inv = eye + jnp.where(level_masks[0], a_ab, 0.0)
            for lvl in range(1, len(level_masks)):
                off = jnp.where(level_masks[lvl], a_ab, 0.0).astype(BF16)
                inv_bf = inv.astype(BF16)
                inv = inv + bmm(bmm(inv_bf, off).astype(BF16), inv_bf)
            x0 = jnp.concatenate([a_t.astype(BF16), bmm(a_ak, v_s).astype(BF16)], axis=2)
            wu = bmm(inv.astype(BF16), x0)
            q = jnp.concatenate([wu.astype(BF16), v_pad], axis=1)
            ry = bmm(m_rbk, q)
            rt_ref[slab] = (r_t + ry[:, :, :LANES]).astype(BF16)
            y_loc = (ry[:, :chunk, LANES:] + ry[:, chunk:, LANES:]).reshape(grows, LANES)
            y_ref[rows, :] = y_loc if assign else y_ref[rows, :] + y_loc
            bk = jnp.concatenate([_stack_heads(b * e_end), _stack_heads(kd * e_end)],
                                 axis=1).astype(BF16)
            mc = jnp.einsum('gkm,gkn->gmn', q[:, :n2, :LANES], bk[:, :n2, :],
                            preferred_element_type=F32)
            dl = jnp.einsum('gkm,gkn->gmn', q[:, :, LANES:], bk, preferred_element_type=F32)
            dec = jnp.exp(tot)
            mc_ref[slab] = mc.astype(BF16)
            dl_ref[slab] = dl
            dec_ref[slab] = jnp.broadcast_to(dec, (grp, 8, LANES))
            first, second = (0, 1) if d == 0 else (1, 0)

            def pick(a, which):
                return a.reshape((grp // 2, 2) + a.shape[1:])[:, which]

            dec_a, dec_b = pick(dec, first), pick(dec, second)
            f1 = pick(mc, first) + eye * dec_a
            fd = jnp.concatenate([f1, pick(dl, first)], axis=1).astype(BF16)
            prod = bmm(fd, pick(mc, second).astype(BF16))
            mc2_ref[pairs] = (prod[:, :LANES, :] + pick(mc, first) * dec_b).astype(BF16)
            dl2_ref[pairs] = prod[:, LANES:, :] + pick(dl, first) * dec_b + pick(dl, second)
            dec2_ref[pairs] = jnp.broadcast_to(dec_a * dec_b, (grp // 2, 8, LANES))

    n_pairs = n_chunks // 2
    assert n_chunks == 2 * grp

    def add_y(rt_ref, c, s_bf):
        if isinstance(c, int):
            rows = slice(c * chunk, (c + 1) * chunk)
        else:
            rows = pl.ds(pl.multiple_of(c * chunk, chunk), chunk)
        y = _dot_nt(rt_ref[c], s_bf)
        y_ref[rows, :] = y_ref[rows, :] + (y[:chunk] + y[chunk:])

    def second_chunk(d, pair):
        return 2 * pair + 1 if d == 0 else 2 * pair

    def phase2(step, carry):
        new = []
        for d, (*_ins, rt_ref, mc_ref, dl_ref, dec_ref, mc2_ref, dl2_ref, dec2_ref) in enumerate(dirs):
            pair = step if d == 0 else n_pairs - 1 - step
            if isinstance(step, int):
                prev = max(step - 1, 0) if d == 0 else min(n_pairs - step, n_pairs - 1)
            else:
                prev = (jnp.maximum(step - 1, 0) if d == 0
                        else jnp.minimum(n_pairs - step, n_pairs - 1))
            c1 = 2 * pair if d == 0 else 2 * pair + 1
            s0, mid_prev = carry[d]
            s0_bf = s0.astype(BF16)
            s_next = s0 * dec2_ref[pair][0:1, :] + _dot(s0_bf, mc2_ref[pair]) + dl2_ref[pair]
            mid = s0 * dec_ref[c1][0:1, :] + _dot(s0_bf, mc_ref[c1]) + dl_ref[c1]
            add_y(rt_ref, c1, s0_bf)
            add_y(rt_ref, second_chunk(d, prev), mid_prev)
            new.append((s_next, mid.astype(BF16)))
        return tuple(new)

    phase1([(0, 0, True), (1, 1, True)])
    zero = jnp.zeros((LANES, LANES), F32)
    carry = ((zero, zero.astype(BF16)),) * 2
    for step in range(n_pairs // 2):
        carry = phase2(step, carry)
    phase1([(0, 1, False), (1, 0, False)])
    final = lax.fori_loop(n_pairs // 2, n_pairs, phase2, carry, unroll=8)
    for d, (*_ins, rt_ref, _mc, _dl, _dec, _mc2, _dl2, _dec2) in enumerate(dirs):
        last_pair = n_pairs - 1 if d == 0 else 0
        add_y(rt_ref, second_chunk(d, last_pair), final[d][1])


def _wkv(r, v, kk, kd0, kd1, ag0, ag1, lw0, lw1, b):
    spec = pl.BlockSpec((None, SEQ, LANES), lambda bi, hp: (bi, 0, hp))
    args = [a.reshape(b, SEQ, C) for a in (r, v, kk, kd0, kd1, ag0, ag1, lw0, lw1)]
    n_chunks = SEQ // WKV_CHUNK
    kern = functools.partial(_wkv_kernel, chunk=WKV_CHUNK, n_chunks=n_chunks)
    per_dir = [pltpu.VMEM((n_chunks, LANES, LANES), BF16),
               pltpu.VMEM((n_chunks, LANES, LANES), BF16),
               pltpu.VMEM((n_chunks, LANES, LANES), F32),
               pltpu.VMEM((n_chunks, 8, LANES), F32),
               pltpu.VMEM((n_chunks // 2, LANES, LANES), BF16),
               pltpu.VMEM((n_chunks // 2, LANES, LANES), F32),
               pltpu.VMEM((n_chunks // 2, 8, LANES), F32)]
    scratch = per_dir + per_dir
    return pl.pallas_call(
        kern,
        grid=(b, C // LANES),
        in_specs=[spec] * 9,
        out_specs=spec,
        out_shape=jax.ShapeDtypeStruct((b, SEQ, C), F32),
        scratch_shapes=scratch,
        compiler_params=_params("parallel", "parallel"),
        name="wkv7",
    )(*args)


def _rwkv_post_kernel(y_ref, r_ref, kd0_ref, kd1_ref, v_ref, g_ref, x_ref, lng_ref, lnb_ref,
                      rk_ref, wo_ref, o_ref):
    ones = _head_ones(COL_CHUNK)
    y = y_ref[...]
    dev = y - _head_sum(y, ones) * (1.0 / HD)
    var = _head_sum(dev * dev, ones) * (1.0 / HD)
    yn = dev * lax.rsqrt(var + RW_GN_EPS) * lng_ref[...] + lnb_ref[...]
    k_bonus = 0.5 * (kd0_ref[...].astype(F32) + kd1_ref[...].astype(F32))
    bonus = _head_sum(r_ref[...].astype(F32) * k_bonus * rk_ref[...], ones) * v_ref[...].astype(F32)
    out = ((yn + bonus) * g_ref[...].astype(F32)).astype(BF16)
    o_ref[...] = x_ref[...] + _dot(out, wo_ref[...])


def _rwkv_post(y, r, kd0, kd1, v, g, x, ln_g, ln_b, r_k, w_o, tm=512):
    t = x.shape[0]
    row = pl.BlockSpec((tm, C), lambda i: (i, 0))
    vec = _const_spec((1, C))
    return pl.pallas_call(
        _rwkv_post_kernel,
        grid=(t // tm,),
        in_specs=[row] * 7 + [vec, vec, vec, _const_spec((C, C))],
        out_specs=row,
        out_shape=jax.ShapeDtypeStruct((t, C), F32),
        compiler_params=_params("parallel"),
        name="rwkv_post",
    )(y, r, kd0, kd1, v, g, x, ln_g.reshape(1, C), ln_b.reshape(1, C), r_k.reshape(1, C),
      w_o.astype(BF16))


def _lambda_init(layer_idx):
    return 0.8 - 0.6 * math.exp(-0.3 * layer_idx)


def _head_gain(q_gain, n_q, k_gain, n_k, n_v):
    scale = HD ** -0.5 * LOG2E
    return jnp.concatenate([jnp.tile(q_gain * scale, n_q // HD), jnp.tile(k_gain, n_k // HD),
                            jnp.ones((n_v,), F32)])


def _encode(x, b, p):
    t = b * SEQ
    x = x.reshape(t, C)
    pos = jnp.arange(SEQ, dtype=F32)

    hn = _head_gain(p['a_q_norm'], C, p['a_k_norm'], C, C)
    qkv = _proj(x, p['n0_attn'], p['a_w_qkv'].astype(BF16), hn, 2 * C,
                rope=_rope_tables(pos, None, HD // 2), rope_half=HD // 2)
    lam_init = _lambda_init(0)
    lam = (jnp.exp(jnp.sum(p['a_lq1'] * p['a_lk1'])) - jnp.exp(jnp.sum(p['a_lq2'] * p['a_lk2']))
           + lam_init).reshape(1)
    o = _diff_attn(qkv.reshape(b, SEQ, 3 * C), lam,
                   (p['a_subln'] * (1.0 - lam_init)).reshape(1, LANES), b)
    x = _out_proj(o.reshape(t, C), p['a_w_o'].astype(BF16), x)
    x = _ffn(x, p['n0_ffn'], p['f0_w_in'], p['f0_conv_w'], p['f0_conv_b'], p['f0_w_out'])

    kvw = GQ_KV * HD
    w = p['b_w_qkv']

    def dup(wc):
        wc = wc.reshape(C, GQ_KV, 1, HD)
        return jnp.broadcast_to(wc, (C, GQ_KV, 2, HD)).reshape(C, 2 * kvw)

    w_dup = jnp.concatenate([w[:, :C], dup(w[:, C:C + kvw]), dup(w[:, C + kvw:])], axis=1)
    hn = _head_gain(p['b_q_norm'], C, p['b_k_norm'], 2 * kvw, 2 * kvw)
    tok = jnp.arange(SEQ)
    rope = _rope_tables((tok // GRID_W).astype(F32), (tok % GRID_W).astype(F32), HD // 4)
    qkv = _proj(x, p['n1_attn'], w_dup.astype(BF16), hn, C + 2 * kvw, rope=rope,
                rope_half=HD // 4)
    o = _gqa(qkv.reshape(b, SEQ, C + 4 * kvw), b)
    x = _out_proj(o.reshape(t, C), p['b_w_o'].astype(BF16), x)
    x = _ffn(x, p['n1_ffn'], p['f1_w_in'], p['f1_conv_w'], p['f1_conv_b'], p['f1_w_out'])

    hn = _head_gain(p['c_q_norm'], C, p['c_k_norm'], C, C)
    qkv = _proj(x, p['n2_attn'], p['c_w_qkv'].astype(BF16), hn, 2 * C)
    o = _na_attn(qkv.reshape(b, SEQ, 3 * C), _na_bias_table(p['c_rel_bias']), b)
    x = _out_proj(o.reshape(t, C), p['c_w_o'].astype(BF16), x)
    x = _ffn(x, p['n2_ffn'], p['f2_w_in'], p['f2_conv_w'], p['f2_conv_b'], p['f2_w_out'])

    r, v, g, kk, kd0, kd1, ag0, ag1, lw0, lw1 = _rwkv_pre(
        x, p['n3_attn'], p['d_mu'], p['d_w_r'], p['d_w_k'], p['d_w_v'], p['d_g1'], p['d_g2'],
        p['d_k_k'], p['d_k_a'], p['d_w0'], p['d_w1'], p['d_w2'], p['d_a0'], p['d_a1'], p['d_a2'])
    y = _wkv(r, v, kk, kd0, kd1, ag0, ag1, lw0, lw1, b)
    x = _rwkv_post(y.reshape(t, C), r, kd0, kd1, v, g, x, p['d_ln_g'], p['d_ln_b'], p['d_r_k'],
                   p['d_w_o'])
    x = _ffn(x, p['n3_ffn'], p['f3_w_in'], p['f3_conv_w'], p['f3_conv_b'], p['f3_w_out'])
    return x.reshape(b, SEQ, C)


def kernel(x_prompt, x_sample,
           n0_attn, n0_ffn, n1_attn, n1_ffn, n2_attn, n2_ffn, n3_attn, n3_ffn, a_w_qkv,
           a_q_norm, a_k_norm, a_lq1, a_lk1, a_lq2, a_lk2, a_subln, a_w_o, b_w_qkv,
           b_q_norm, b_k_norm, b_w_o, c_w_qkv, c_q_norm, c_k_norm, c_rel_bias, c_w_o, d_mu,
           d_w_r, d_w_k, d_w_v, d_w_o, d_g1, d_g2, d_k_k, d_k_a, d_r_k, d_ln_g, d_ln_b,
           d_w0, d_w1, d_w2, d_a0, d_a1, d_a2, f0_w_in, f0_conv_w, f0_conv_b, f0_w_out,
           f1_w_in, f1_conv_w, f1_conv_b, f1_w_out, f2_w_in, f2_conv_w, f2_conv_b, f2_w_out,
           f3_w_in, f3_conv_w, f3_conv_b, f3_w_out):
    p = dict(locals())
    xs = (p.pop('x_prompt'), p.pop('x_sample'))
    return tuple(_encode(x, x.shape[0], p) for x in xs)
```

```python
import functools
import math

import numpy as np
import jax
import jax.numpy as jnp
from jax import lax
from jax.experimental import pallas as pl
from jax.experimental.pallas import tpu as pltpu

F32 = jnp.float32
BF16 = jnp.bfloat16

C = 1024
HD = 64
LANES = 128
SEQ = 2048
GRID_W = 64
ROPE_THETA = 10000.0
NORM_EPS = 1e-6
DA_HEADS = 8
DA_SUBLN_EPS = 1e-5
GQ_KV = 4
NA_WIN_ROWS = 8
NA_WIN_COLS = 16
NA_QROWS = 4
NA_KROWS = 12
RW_GN_EPS = 64e-5
FFN_HIDDEN = 2816
WKV_CHUNK = 64
LOG2E = math.log2(math.e)

COL_CHUNK = 256
VMEM_LIMIT = 56 * 1024 * 1024


def _params(*sem):
    return pltpu.CompilerParams(dimension_semantics=sem, vmem_limit_bytes=VMEM_LIMIT)


def _rms(x, g, eps):
    return x * lax.rsqrt(jnp.mean(x * x, axis=-1, keepdims=True) + eps) * g


def _split(x):
    hi = x.astype(BF16)
    lo = (x - hi.astype(F32)).astype(BF16)
    return hi, lo


def _head_ones(n):
    r = lax.broadcasted_iota(jnp.int32, (n, n), 0)
    c = lax.broadcasted_iota(jnp.int32, (n, n), 1)
    return jnp.where((r // HD) == (c // HD), 1.0, 0.0).astype(BF16)


def _head_sum(x, ones):
    w = ones.shape[0]
    out = []
    for c in range(x.shape[1] // w):
        hi, lo = _split(x[:, c * w:(c + 1) * w])
        out.append(_dot(hi, ones) + _dot(lo, ones))
    return out[0] if len(out) == 1 else jnp.concatenate(out, axis=1)


def _dot(a, b):
    return jnp.dot(a, b, preferred_element_type=F32)


def _dot_nt(a, b):
    return lax.dot_general(a, b, (((1,), (1,)), ((), ())), preferred_element_type=F32)


def _shift_rows(u, prev_row, next_row):
    n = u.shape[0]
    up = pltpu.roll(u, 1, 0)
    dn = pltpu.roll(u, n - 1, 0)
    row = lax.broadcasted_iota(jnp.int32, (8, u.shape[1]), 0)
    up = jnp.concatenate([jnp.where(row == 0, prev_row, up[:8]), up[8:]], axis=0)
    dn = jnp.concatenate([dn[:n - 8], jnp.where(row == 7, next_row, dn[n - 8:])], axis=0)
    return up, dn


def _lo_mask(shape):
    return lax.broadcasted_iota(jnp.int32, shape, len(shape) - 1) < HD


def _stack_heads(x):
    m = _lo_mask(x.shape)
    zero = jnp.zeros_like(x)
    return jnp.concatenate([jnp.where(m, x, zero), jnp.where(m, zero, x)], axis=-2)


def _proj_kernel(*refs, n_chunks, n_norm_chunks, rope_half):
    if rope_half:
        x_ref, g_ref, w_ref, hn_ref, cos_ref, sin_ref, o_ref = refs
    else:
        x_ref, g_ref, w_ref, hn_ref, o_ref = refs
    xn = _rms(x_ref[...], g_ref[...], NORM_EPS).astype(BF16)
    ones = _head_ones(COL_CHUNK)
    if rope_half:
        lane = lax.broadcasted_iota(jnp.int32, (1, COL_CHUNK), 1)
        first = (lane % (2 * rope_half)) < rope_half
    def mm(c):
        return _dot(xn, w_ref[:, c * COL_CHUNK:(c + 1) * COL_CHUNK])

    y_next = mm(0)
    for c in range(n_chunks):
        cols = slice(c * COL_CHUNK, (c + 1) * COL_CHUNK)
        y = y_next
        if c + 1 < n_chunks:
            y_next = mm(c + 1)
        if c < n_norm_chunks:
            ms = _dot((y * y).astype(BF16), ones) * (1.0 / HD)
            y = y * lax.rsqrt(ms + NORM_EPS) * hn_ref[:, cols]
            if rope_half:
                rot = jnp.where(first, pltpu.roll(y, COL_CHUNK - rope_half, 1),
                                pltpu.roll(y, rope_half, 1))
                y = y * cos_ref[...] + rot * sin_ref[...]
        o_ref[:, cols] = y.astype(BF16)


def _proj(x, g, w, hn, n_norm_cols, rope=None, rope_half=0, tm=512):
    t = x.shape[0]
    n = w.shape[1]
    tiles_per_seq = SEQ // tm
    in_specs = [
        pl.BlockSpec((tm, C), lambda i: (i, 0)),
        pl.BlockSpec((1, C), lambda i: (0, 0)),
        pl.BlockSpec((C, n), lambda i: (0, 0)),
        pl.BlockSpec((1, n), lambda i: (0, 0)),
    ]
    args = [x, g.reshape(1, C), w, hn.reshape(1, n)]
    if rope_half:
        spec = pl.BlockSpec((tm, COL_CHUNK), lambda i: (i % tiles_per_seq, 0))
        in_specs += [spec, spec]
        args += [rope[0], rope[1]]
    kern = functools.partial(_proj_kernel, n_chunks=n // COL_CHUNK,
                             n_norm_chunks=n_norm_cols // COL_CHUNK, rope_half=rope_half)
    return pl.pallas_call(
        kern,
        grid=(t // tm,),
        in_specs=in_specs,
        out_specs=pl.BlockSpec((tm, n), lambda i: (i, 0)),
        out_shape=jax.ShapeDtypeStruct((t, n), BF16),
        compiler_params=_params("parallel"),
        name="proj",
    )(*args)


def _rope_tables(pos_first, pos_second, half):
    inv = ROPE_THETA ** (-jnp.arange(half, dtype=F32) / half)

    def one(pos):
        ang = pos[:, None] * inv[None, :]
        cos = jnp.concatenate([jnp.cos(ang), jnp.cos(ang)], axis=-1)
        sin = jnp.concatenate([-jnp.sin(ang), jnp.sin(ang)], axis=-1)
        return cos, sin

    c1, s1 = one(pos_first)
    if pos_second is None:
        cos, sin = c1, s1
    else:
        c2, s2 = one(pos_second)
        cos = jnp.concatenate([c1, c2], axis=-1)
        sin = jnp.concatenate([s1, s2], axis=-1)
    reps = COL_CHUNK // cos.shape[-1]
    return jnp.tile(cos, (1, reps)), jnp.tile(sin, (1, reps))


def _out_kernel(a_ref, w_ref, x_ref, o_ref):
    o_ref[...] = x_ref[...] + _dot(a_ref[...], w_ref[...])


def _out_proj(a, w, x, tm=1024):
    t = x.shape[0]
    return pl.pallas_call(
        _out_kernel,
        grid=(t // tm,),
        in_specs=[pl.BlockSpec((tm, C), lambda i: (i, 0)),
                  pl.BlockSpec((C, C), lambda i: (0, 0)),
                  pl.BlockSpec((tm, C), lambda i: (i, 0))],
        out_specs=pl.BlockSpec((tm, C), lambda i: (i, 0)),
        out_shape=jax.ShapeDtypeStruct((t, C), F32),
        compiler_params=_params("parallel"),
        name="out_proj",
    )(a, w, x)


HALO = 8


def _halo_rows(xp_ref, xn_ref, tiles_per_seq):
    i = pl.program_id(0)
    pos = i % tiles_per_seq
    halo = jnp.concatenate([xp_ref[...], xn_ref[...]], axis=0)
    row = lax.broadcasted_iota(jnp.int32, halo.shape, 0)
    has_prev = (pos != 0).astype(jnp.int32)
    has_next = (pos != tiles_per_seq - 1).astype(jnp.int32)
    keep = jnp.where(row < HALO, has_prev, has_next)
    return jnp.where(keep != 0, halo, 0.0)


def _ffn_kernel(x_ref, xp_ref, xn_ref, g_ref, wi_ref, cw_ref, cb_ref, wo_ref, o_ref, hid_ref, *,
                tm, tiles_per_seq):
    x = x_ref[...]
    xe = jnp.concatenate([x, _halo_rows(xp_ref, xn_ref, tiles_per_seq)], axis=0)
    xne = _rms(xe, g_ref[...], NORM_EPS).astype(BF16)

    def in_proj(c):
        gate_cols = slice(c * COL_CHUNK, (c + 1) * COL_CHUNK)
        val_cols = slice(FFN_HIDDEN + c * COL_CHUNK, FFN_HIDDEN + (c + 1) * COL_CHUNK)
        return _dot(xne, wi_ref[:, gate_cols]), _dot(xne, wi_ref[:, val_cols])

    def conv(u, cols):
        um = u[:tm]
        up, dn = _shift_rows(um, u[tm + HALO - 1:tm + HALO], u[tm + HALO:tm + HALO + 1])
        cw = cw_ref[:, cols]
        return up * cw[0:1] + um * cw[1:2] + dn * cw[2:3] + cb_ref[:, cols]

    n_chunks = FFN_HIDDEN // COL_CHUNK
    u_next = in_proj(0)
    for c in range(n_chunks):
        ug, uv = u_next
        if c + 1 < n_chunks:
            u_next = in_proj(c + 1)
        cols = slice(c * COL_CHUNK, (c + 1) * COL_CHUNK)
        gate = conv(ug, cols)
        val = conv(uv, slice(FFN_HIDDEN + c * COL_CHUNK, FFN_HIDDEN + (c + 1) * COL_CHUNK))
        hid_ref[:, cols] = (gate * jax.nn.sigmoid(gate) * val).astype(BF16)
    o_ref[...] = x + _dot(hid_ref[...], wo_ref[...])


def _halo_specs(tm, t):
    per = tm // HALO
    last = t // HALO - 1
    return [pl.BlockSpec((HALO, C), lambda i: (jnp.maximum(i * per - 1, 0), 0)),
            pl.BlockSpec((HALO, C), lambda i: (jnp.minimum((i + 1) * per, last), 0))]


def _const_spec(shape):
    return pl.BlockSpec(shape, lambda i: (0,) * len(shape), pipeline_mode=pl.Buffered(1))


def _ffn(x, g, w_in, conv_w, conv_b, w_out, tm=512):
    t = x.shape[0]
    kern = functools.partial(_ffn_kernel, tm=tm, tiles_per_seq=SEQ // tm)
    return pl.pallas_call(
        kern,
        grid=(t // tm,),
        in_specs=[pl.BlockSpec((tm, C), lambda i: (i, 0))] + _halo_specs(tm, t) + [
            _const_spec((1, C)),
            _const_spec((C, 2 * FFN_HIDDEN)),
            _const_spec((3, 2 * FFN_HIDDEN)),
            _const_spec((1, 2 * FFN_HIDDEN)),
            _const_spec((FFN_HIDDEN, C))],
        out_specs=pl.BlockSpec((tm, C), lambda i: (i, 0)),
        out_shape=jax.ShapeDtypeStruct((t, C), F32),
        scratch_shapes=[pltpu.VMEM((tm, FFN_HIDDEN), BF16)],
        compiler_params=_params("parallel"),
        name="conv_ffn",
    )(x, x, x, g.reshape(1, C), w_in.astype(BF16), conv_w, conv_b.reshape(1, -1),
      w_out.astype(BF16))


ATTN_SUB = 256


def _softmax_rows(s):
    p = jnp.exp2(s - jnp.max(s, axis=-1, keepdims=True))
    return p, 1.0 / jnp.sum(p, axis=-1, keepdims=True)


def _diff_attn_kernel(lam_ref, q_ref, k_ref, v_ref, sg_ref, o_ref, *, tq):
    n = ATTN_SUB // 2
    n_tiles = tq // n

    def scores(t):
        return _dot_nt(_stack_heads(q_ref[t * n:(t + 1) * n, :]), k_ref[...])

    s_next = scores(0)
    for t in range(n_tiles):
        s = s_next
        if t + 1 < n_tiles:
            s_next = scores(t + 1)
        p, inv = _softmax_rows(s)
        o2 = _dot(p.astype(BF16), v_ref[...]) * inv
        o = o2[:n] - lam_ref[0] * o2[n:]
        o_ref[t * n:(t + 1) * n, :] = _rms(o, sg_ref[...], DA_SUBLN_EPS).astype(BF16)


def _diff_attn(qkv, lam, subln_gain, b, tq=2048):
    q_blocks = C // LANES
    kern = functools.partial(_diff_attn_kernel, tq=tq)
    return pl.pallas_call(
        kern,
        grid=(b, DA_HEADS, SEQ // tq),
        in_specs=[pl.BlockSpec(memory_space=pltpu.SMEM),
                  pl.BlockSpec((None, tq, LANES), lambda bi, h, qi: (bi, qi, h)),
                  pl.BlockSpec((None, SEQ, LANES), lambda bi, h, qi: (bi, 0, q_blocks + h)),
                  pl.BlockSpec((None, SEQ, LANES), lambda bi, h, qi: (bi, 0, 2 * q_blocks + h)),
                  pl.BlockSpec((1, LANES), lambda bi, h, qi: (0, 0))],
        out_specs=pl.BlockSpec((None, tq, LANES), lambda bi, h, qi: (bi, qi, h)),
        out_shape=jax.ShapeDtypeStruct((b, SEQ, C), BF16),
        compiler_params=_params("parallel", "parallel", "parallel"),
        name="diff_attn",
    )(lam, qkv, qkv, qkv, subln_gain)


def _gqa_kernel(q_ref, k_ref, v_ref, o_ref, *, tq):
    n = ATTN_SUB // 2
    m = _lo_mask((n, LANES))
    tiles = [(slice(t * n, (t + 1) * n), slice(half * LANES, (half + 1) * LANES))
             for t in range(tq // n) for half in range(2)]

    def scores(tile):
        return _dot_nt(_stack_heads(q_ref[tile]), k_ref[...])

    s_next = scores(tiles[0])
    for i, tile in enumerate(tiles):
        s = s_next
        if i + 1 < len(tiles):
            s_next = scores(tiles[i + 1])
        p, inv = _softmax_rows(s)
        o = _dot(p.astype(BF16), v_ref[...]) * inv
        o_ref[tile] = jnp.where(m, o[:n], o[n:]).astype(BF16)


def _gqa(qkv, b, tq=1024):
    kern = functools.partial(_gqa_kernel, tq=tq)
    k0 = C // LANES
    return pl.pallas_call(
        kern,
        grid=(b, GQ_KV, SEQ // tq),
        in_specs=[pl.BlockSpec((None, tq, 2 * LANES), lambda bi, j, qi: (bi, qi, j)),
                  pl.BlockSpec((None, SEQ, LANES), lambda bi, j, qi: (bi, 0, k0 + j)),
                  pl.BlockSpec((None, SEQ, LANES), lambda bi, j, qi: (bi, 0, k0 + GQ_KV + j))],
        out_specs=pl.BlockSpec((None, tq, 2 * LANES), lambda bi, j, qi: (bi, qi, j)),
        out_shape=jax.ShapeDtypeStruct((b, SEQ, C), BF16),
        compiler_params=_params("parallel", "parallel", "parallel"),
        name="gqa_attn",
    )(qkv, qkv, qkv)


NA_TQ = NA_QROWS * GRID_W
NA_TK = NA_KROWS * GRID_W
NA_KBLK = NA_TK // NA_TQ
NA_STEPS = SEQ // NA_TQ
NA_SUB = 512
NA_BATCH = 8


def _na_kernel(q_ref, k0_ref, k1_ref, k2_ref, v0_ref, v1_ref, v2_ref, bias_ref, o_ref, *, nb):
    n = NA_SUB // 2
    m = _lo_mask((n, LANES))
    tiles = [(bi, t) for bi in range(nb) for t in range(NA_TQ // n)]

    def scores(tile):
        bi, t = tile
        k = jnp.concatenate([k0_ref[bi], k1_ref[bi], k2_ref[bi]], axis=0)
        return _dot_nt(_stack_heads(q_ref[bi, t * n:(t + 1) * n, :]), k)

    s_next = scores(tiles[0])
    for i, (bi, t) in enumerate(tiles):
        s = s_next
        if i + 1 < len(tiles):
            s_next = scores(tiles[i + 1])
        rows = slice(t * n, (t + 1) * n)
        bias = jnp.concatenate([bias_ref[rows, :],
                                bias_ref[NA_TQ + t * n:NA_TQ + (t + 1) * n, :]], axis=0)
        p, inv = _softmax_rows(s + bias)
        v = jnp.concatenate([v0_ref[bi], v1_ref[bi], v2_ref[bi]], axis=0)
        o = _dot(p.astype(BF16), v) * inv
        o_ref[bi, rows, :] = jnp.where(m, o[:n], o[n:]).astype(BF16)


def _na_key_block(i):
    return jnp.clip(i - 1, 0, NA_STEPS - NA_KBLK)


def _na_pattern(i):
    return jnp.minimum(i, 2) + i // (NA_STEPS - 1)


def _na_attn(qkv, bias, b):
    nblk = C // LANES
    nb = math.gcd(b, NA_BATCH)

    def kv_spec(base, j):
        return pl.BlockSpec((nb, NA_TQ, LANES),
                            lambda i, hp, bi: (bi, _na_key_block(i) + j, base + hp))

    return pl.pallas_call(
        functools.partial(_na_kernel, nb=nb),
        grid=(NA_STEPS, nblk, b // nb),
        in_specs=[pl.BlockSpec((nb, NA_TQ, LANES), lambda i, hp, bi: (bi, i, hp))]
        + [kv_spec(nblk, j) for j in range(NA_KBLK)]
        + [kv_spec(2 * nblk, j) for j in range(NA_KBLK)]
        + [pl.BlockSpec((None, None, 2 * NA_TQ, NA_TK),
                        lambda i, hp, bi: (_na_pattern(i), hp, 0, 0))],
        out_specs=pl.BlockSpec((nb, NA_TQ, LANES), lambda i, hp, bi: (bi, i, hp)),
        out_shape=jax.ShapeDtypeStruct((b, SEQ, C), BF16),
        compiler_params=_params("parallel", "parallel", "parallel"),
        name="na_attn",
    )(qkv, qkv, qkv, qkv, qkv, qkv, qkv, bias)


def _na_bias_table(rel_bias):
    rows = SEQ // GRID_W
    wr, wc = NA_WIN_ROWS, NA_WIN_COLS
    heads = rel_bias.shape[0]
    cols = np.arange(GRID_W)
    col_start = np.clip(cols - wc // 2, 0, GRID_W - wc)
    col_mask = (cols[None, :] >= col_start[:, None]) & (cols[None, :] < col_start[:, None] + wc)
    dc = np.clip(cols[None, :] - cols[:, None] + (wc - 1), 0, 2 * wc - 2)
    onehot = (dc.reshape(-1)[None, :] == np.arange(2 * wc - 1)[:, None]).astype(np.float32)
    tiles = jnp.dot(rel_bias.reshape(heads * (2 * wr - 1), 2 * wc - 1), jnp.asarray(onehot),
                    precision=lax.Precision.HIGHEST)
    tiles = tiles.reshape(heads, 2 * wr - 1, GRID_W, GRID_W)
    tiles = jnp.where(jnp.asarray(col_mask)[None, None], tiles * LOG2E, -1e30)
    tiles = jnp.concatenate([tiles, jnp.full((heads, 1, GRID_W, GRID_W), -1e30, F32)], axis=1)
    n_dr = 2 * wr - 1
    idx = np.full((4, NA_QROWS, NA_KROWS), n_dr, np.int32)
    starts = [(0, 0), (NA_QROWS, 0), (2 * NA_QROWS, NA_QROWS), (rows - NA_QROWS, rows - NA_KROWS)]
    for p, (r0, ks) in enumerate(starts):
        for a in range(NA_QROWS):
            r = r0 + a
            rs = min(max(r - wr // 2, 0), rows - wr)
            for bb in range(NA_KROWS):
                j = ks + bb
                if rs <= j < rs + wr:
                    idx[p, a, bb] = j - r + (wr - 1)
    big = tiles[:, jnp.asarray(idx)]
    big = big.transpose(1, 0, 2, 4, 3, 5)
    return big.reshape(4, heads // 2, 2 * NA_TQ, NA_TK)


def _rwkv_pre_kernel(x_ref, xp_ref, xn_ref, g_ref, mu_ref, wr_ref, wk_ref, wv_ref, g1_ref, g2_ref,
                     w1_ref, w2_ref, a1_ref, a2_ref, w0_ref, a0_ref, kk_ref, ka_ref,
                     r_o, v_o, g_o, kk_o, kd0_o, kd1_o, ag0_o, ag1_o, lw0_o, lw1_o, *,
                     tiles_per_seq):
    g = g_ref[...]
    h = _rms(x_ref[...], g, NORM_EPS)
    hh = _rms(_halo_rows(xp_ref, xn_ref, tiles_per_seq), g, NORM_EPS)
    up, dn = _shift_rows(h, hh[HALO - 1:HALO], hh[HALO:HALO + 1])
    xx = 0.5 * (up + dn) - h

    def mix(j):
        return (h + xx * mu_ref[j:j + 1, :]).astype(BF16)

    tw = jnp.tanh(_dot(mix(1), w1_ref[...]))
    ta = _dot(mix(4), a1_ref[...])
    gate = jax.nn.sigmoid(_dot(mix(5), g1_ref[...]))
    lo = _lo_mask(tw.shape)
    sels = (lo, jnp.logical_not(lo))
    zs = [_dot(jnp.where(s, tw, 0.0).astype(BF16), w2_ref[...]) for s in sels]
    az = [_dot(jnp.where(s, ta, 0.0).astype(BF16), a2_ref[...]) for s in sels]
    k = _dot(mix(2), wk_ref[...])
    r_o[...] = _dot(mix(0), wr_ref[...]).astype(BF16)
    v_o[...] = _dot(mix(3), wv_ref[...]).astype(BF16)
    g_o[...] = _dot(gate.astype(BF16), g2_ref[...]).astype(BF16)

    kkv = k * kk_ref[...]
    sq = _head_sum(kkv * kkv, _head_ones(COL_CHUNK))
    kk_o[...] = (kkv * lax.rsqrt(jnp.maximum(sq, 1e-24))).astype(BF16)
    ka = ka_ref[...]
    for d, (kd_o, ag_o, lw_o) in enumerate(((kd0_o, ag0_o, lw0_o), (kd1_o, ag1_o, lw1_o))):
        lw_o[...] = -math.exp(-0.5) * jax.nn.sigmoid(w0_ref[d:d + 1, :] + zs[d])
        a = jax.nn.sigmoid(a0_ref[d:d + 1, :] + az[d])
        ag_o[...] = a.astype(BF16)
        kd_o[...] = (k * (1.0 + (a - 1.0) * ka)).astype(BF16)


def _rwkv_pre(x, g, mu, w_r, w_k, w_v, g1, g2, k_k, k_a, w0, w1, w2, a0, a1, a2, tm=512):
    t = x.shape[0]
    lora = w1.shape[-1]
    w1c = jnp.concatenate([w1[0], w1[1]], axis=1).astype(BF16)
    a1c = jnp.concatenate([a1[0], a1[1]], axis=1).astype(BF16)
    w2c = w2.reshape(2 * lora, C).astype(BF16)
    a2c = a2.reshape(2 * lora, C).astype(BF16)
    assert 2 * lora == LANES
    glora = g1.shape[-1]
    row = pl.BlockSpec((tm, C), lambda i: (i, 0))
    kern = functools.partial(_rwkv_pre_kernel, tiles_per_seq=SEQ // tm)
    bf = jax.ShapeDtypeStruct((t, C), BF16)
    f32 = jax.ShapeDtypeStruct((t, C), F32)
    return pl.pallas_call(
        kern,
        grid=(t // tm,),
        in_specs=[row] + _halo_specs(tm, t) + [
            _const_spec((1, C)), _const_spec((6, C)),
            _const_spec((C, C)), _const_spec((C, C)), _const_spec((C, C)),
            _const_spec((C, glora)), _const_spec((glora, C)),
            _const_spec((C, LANES)), _const_spec((LANES, C)),
            _const_spec((C, LANES)), _const_spec((LANES, C)),
            _const_spec((2, C)), _const_spec((2, C)), _const_spec((1, C)), _const_spec((1, C))],
        out_specs=[row] * 10,
        out_shape=[bf] * 8 + [f32] * 2,
        compiler_params=_params("parallel"),
        name="rwkv_pre",
    )(x, x, x, g.reshape(1, C), mu, w_r.astype(BF16), w_k.astype(BF16), w_v.astype(BF16),
      g1.astype(BF16), g2.astype(BF16), w1c, w2c, a1c, a2c, w0, a0,
      k_k.reshape(1, C), k_a.reshape(1, C))


WKV_GROUP = 16


def _wkv_kernel(r_ref, v_ref, kk_ref, kd0_ref, kd1_ref, ag0_ref, ag1_ref, lw0_ref, lw1_ref,
                y_ref, *scratch, chunk, n_chunks):
    n2 = 2 * chunk
    ri = lax.broadcasted_iota(jnp.int32, (n2, 2 * n2), 0)
    ci = lax.broadcasted_iota(jnp.int32, (n2, 2 * n2), 1)
    same_head = (ri // chunk) == ((ci % n2) // chunk)
    tr, tc = ri % chunk, ci % chunk
    qi = lax.broadcasted_iota(jnp.int32, (n2, n2), 0)
    qj = lax.broadcasted_iota(jnp.int32, (n2, n2), 1)
    eye = jnp.where(qi == qj, 1.0, 0.0)
    level_masks = [(qi // 2) == (qj // 2)]
    size = 2
    while size < chunk:
        level_masks.append(((qi // (2 * size)) == (qj // (2 * size)))
                           & ((qi // size) != (qj // size)))
        size *= 2
    n_scr = len(scratch) // 2
    dirs = ((kd0_ref, ag0_ref, lw0_ref) + tuple(scratch[:n_scr]),
            (kd1_ref, ag1_ref, lw1_ref) + tuple(scratch[n_scr:]))

    def chunk_prefix(x):
        pos = lax.broadcasted_iota(jnp.int32, x.shape, 0) % chunk
        shift = 1
        while shift < chunk:
            x = x + jnp.where(pos >= shift, pltpu.roll(x, shift, 0), 0.0)
            shift *= 2
        return x

    masks = ((same_head & (tr > tc), same_head & (tr >= tc)),
             (same_head & (tr < tc), same_head & (tr <= tc)))

    grp = WKV_GROUP
    grows = grp * chunk

    def bmm(a, b):
        return jnp.einsum('gmk,gkn->gmn', a, b, preferred_element_type=F32)


    def phase1(jobs):
        for d, g, assign in jobs:
            rows = slice(g * grows, (g + 1) * grows)
            slab = slice(g * grp, (g + 1) * grp)
            pairs = slice(g * (grp // 2), (g + 1) * (grp // 2))
            (kd_ref, ag_ref, lw_ref, rt_ref, mc_ref, dl_ref, dec_ref,
             mc2_ref, dl2_ref, dec2_ref) = dirs[d]

            def load(ref, rows=rows):
                return ref[rows, :].reshape(grp, chunk, LANES)

            r = load(r_ref).astype(F32)
            kk = load(kk_ref).astype(F32)
            v_s = _stack_heads(load(v_ref))
            v_pad = jnp.concatenate([jnp.zeros((grp, n2, LANES), BF16), v_s], axis=2)
            strict, incl = masks[d]
            lw = lw_ref[rows, :]
            pre = chunk_prefix(lw).reshape(grp, chunk, LANES)
            lw = lw.reshape(grp, chunk, LANES)
            tot = pre[:, chunk - 1:chunk, :]
            if d == 0:
                c_in = pre
                c_ex = pre - lw
            else:
                c_ex = tot - pre
                c_in = c_ex + lw
            kd = load(kd_ref).astype(F32)
            b = kk * load(ag_ref).astype(F32)
            e_neg = jnp.exp(-c_in)
            e_end = jnp.exp(tot - c_in)
            a_t = _stack_heads(-kk * jnp.exp(c_ex))
            r_t = _stack_heads(r * jnp.exp(c_in))
            b_t = (b * e_neg).astype(BF16)
            k_t = (kd * e_neg).astype(BF16)
            lhs = jnp.concatenate([a_t, r_t], axis=1).astype(BF16)
            rhs = jnp.concatenate([b_t, b_t, k_t, k_t], axis=1)
            gram = jnp.einsum('gmd,gnd->gmn', lhs, rhs, preferred_element_type=F32)
            a_abk = jnp.where(strict, gram[:, :n2, :], 0.0)
            a_ab = a_abk[:, :, :n2]
            a_ak = a_abk[:, :, n2:].astype(BF16)
            m_rbk = jnp.where(incl, gram[:, n2:, :], 0.0).astype(BF16)
            inv = eye + jnp.where(level_masks[0], a_ab, 0.0)
            for lvl in range(1, len(level_masks)):
                off = jnp.where(level_masks[lvl], a_ab, 0.0).astype(BF16)
                inv_bf = inv.astype(BF16)
                inv = inv + bmm(bmm(inv_bf, off).astype(BF16), inv_bf)
            x0 = jnp.concatenate([a_t.astype(BF16), bmm(a_ak, v_s).astype(BF16)], axis=2)
            wu = bmm(inv.astype(BF16), x0)
            q = jnp.concatenate([wu.astype(BF16), v_pad], axis=1)
            ry = bmm(m_rbk, q)
            rt_ref[slab] = (r_t + ry[:, :, :LANES]).astype(BF16)
            y_loc = (ry[:, :chunk, LANES:] + ry[:, chunk:, LANES:]).reshape(grows, LANES)
            y_ref[rows, :] = y_loc if assign else y_ref[rows, :] + y_loc
            bk = jnp.concatenate([_stack_heads(b * e_end), _stack_heads(kd * e_end)],
                                 axis=1).astype(BF16)
            mc = jnp.einsum('gkm,gkn->gmn', q[:, :n2, :LANES], bk[:, :n2, :],
                            preferred_element_type=F32)
            dl = jnp.einsum('gkm,gkn->gmn', q[:, :, LANES:], bk, preferred_element_type=F32)
            dec = jnp.exp(tot)
            mc_ref[slab] = mc.astype(BF16)
            dl_ref[slab] = dl
            dec_ref[slab] = jnp.broadcast_to(dec, (grp, 8, LANES))
            first, second = (0, 1) if d == 0 else (1, 0)

            def pick(a, which):
                return a.reshape((grp // 2, 2) + a.shape[1:])[:, which]

            dec_a, dec_b = pick(dec, first), pick(dec, second)
            f1 = pick(mc, first) + eye * dec_a
            fd = jnp.concatenate([f1, pick(dl, first)], axis=1).astype(BF16)
            prod = bmm(fd, pick(mc, second).astype(BF16))
            mc2_ref[pairs] = (prod[:, :LANES, :] + pick(mc, first) * dec_b).astype(BF16)
            dl2_ref[pairs] = prod[:, LANES:, :] + pick(dl, first) * dec_b + pick(dl, second)
            dec2_ref[pairs] = jnp.broadcast_to(dec_a * dec_b, (grp // 2, 8, LANES))

    n_pairs = n_chunks // 2
    assert n_chunks == 2 * grp

    def add_y(rt_ref, c, s_bf):
        if isinstance(c, int):
            rows = slice(c * chunk, (c + 1) * chunk)
        else:
            rows = pl.ds(pl.multiple_of(c * chunk, chunk), chunk)
        y = _dot_nt(rt_ref[c], s_bf)
        y_ref[rows, :] = y_ref[rows, :] + (y[:chunk] + y[chunk:])

    def second_chunk(d, pair):
        return 2 * pair + 1 if d == 0 else 2 * pair

    def phase2(step, carry):
        new = []
        for d, (*_ins, rt_ref, mc_ref, dl_ref, dec_ref, mc2_ref, dl2_ref, dec2_ref) in enumerate(dirs):
            pair = step if d == 0 else n_pairs - 1 - step
            if isinstance(step, int):
                prev = max(step - 1, 0) if d == 0 else min(n_pairs - step, n_pairs - 1)
            else:
                prev = (jnp.maximum(step - 1, 0) if d == 0
                        else jnp.minimum(n_pairs - step, n_pairs - 1))
            c1 = 2 * pair if d == 0 else 2 * pair + 1
            s0, mid_prev = carry[d]
            s0_bf = s0.astype(BF16)
            s_next = s0 * dec2_ref[pair][0:1, :] + _dot(s0_bf, mc2_ref[pair]) + dl2_ref[pair]
            mid = s0 * dec_ref[c1][0:1, :] + _dot(s0_bf, mc_ref[c1]) + dl_ref[c1]
            add_y(rt_ref, c1, s0_bf)
            add_y(rt_ref, second_chunk(d, prev), mid_prev)
            new.append((s_next, mid.astype(BF16)))
        return tuple(new)

    phase1([(0, 0, True), (1, 1, True)])
    zero = jnp.zeros((LANES, LANES), F32)
    carry = ((zero, zero.astype(BF16)),) * 2
    for step in range(n_pairs // 2):
        carry = phase2(step, carry)
    phase1([(0, 1, False), (1, 0, False)])
    final = lax.fori_loop(n_pairs // 2, n_pairs, phase2, carry, unroll=8)
    for d, (*_ins, rt_ref, _mc, _dl, _dec, _mc2, _dl2, _dec2) in enumerate(dirs):
        last_pair = n_pairs - 1 if d == 0 else 0
        add_y(rt_ref, second_chunk(d, last_pair), final[d][1])


def _wkv(r, v, kk, kd0, kd1, ag0, ag1, lw0, lw1, b):
    spec = pl.BlockSpec((None, SEQ, LANES), lambda bi, hp: (bi, 0, hp))
    args = [a.reshape(b, SEQ, C) for a in (r, v, kk, kd0, kd1, ag0, ag1, lw0, lw1)]
    n_chunks = SEQ // WKV_CHUNK
    kern = functools.partial(_wkv_kernel, chunk=WKV_CHUNK, n_chunks=n_chunks)
    per_dir = [pltpu.VMEM((n_chunks, LANES, LANES), BF16),
               pltpu.VMEM((n_chunks, LANES, LANES), BF16),
               pltpu.VMEM((n_chunks, LANES, LANES), F32),
               pltpu.VMEM((n_chunks, 8, LANES), F32),
               pltpu.VMEM((n_chunks // 2, LANES, LANES), BF16),
               pltpu.VMEM((n_chunks // 2, LANES, LANES), F32),
               pltpu.VMEM((n_chunks // 2, 8, LANES), F32)]
    scratch = per_dir + per_dir
    return pl.pallas_call(
        kern,
        grid=(b, C // LANES),
        in_specs=[spec] * 9,
        out_specs=spec,
        out_shape=jax.ShapeDtypeStruct((b, SEQ, C), F32),
        scratch_shapes=scratch,
        compiler_params=_params("parallel", "parallel"),
        name="wkv7",
    )(*args)


def _rwkv_post_kernel(y_ref, r_ref, kd0_ref, kd1_ref, v_ref, g_ref, x_ref, lng_ref, lnb_ref,
                      rk_ref, wo_ref, o_ref):
    ones = _head_ones(COL_CHUNK)
    y = y_ref[...]
    dev = y - _head_sum(y, ones) * (1.0 / HD)
    var = _head_sum(dev * dev, ones) * (1.0 / HD)
    yn = dev * lax.rsqrt(var + RW_GN_EPS) * lng_ref[...] + lnb_ref[...]
    k_bonus = 0.5 * (kd0_ref[...].astype(F32) + kd1_ref[...].astype(F32))
    bonus = _head_sum(r_ref[...].astype(F32) * k_bonus * rk_ref[...], ones) * v_ref[...].astype(F32)
    out = ((yn + bonus) * g_ref[...].astype(F32)).astype(BF16)
    o_ref[...] = x_ref[...] + _dot(out, wo_ref[...])


def _rwkv_post(y, r, kd0, kd1, v, g, x, ln_g, ln_b, r_k, w_o, tm=512):
    t = x.shape[0]
    row = pl.BlockSpec((tm, C), lambda i: (i, 0))
    vec = _const_spec((1, C))
    return pl.pallas_call(
        _rwkv_post_kernel,
        grid=(t // tm,),
        in_specs=[row] * 7 + [vec, vec, vec, _const_spec((C, C))],
        out_specs=row,
        out_shape=jax.ShapeDtypeStruct((t, C), F32),
        compiler_params=_params("parallel"),
        name="rwkv_post",
    )(y, r, kd0, kd1, v, g, x, ln_g.reshape(1, C), ln_b.reshape(1, C), r_k.reshape(1, C),
      w_o.astype(BF16))


def _lambda_init(layer_idx):
    return 0.8 - 0.6 * math.exp(-0.3 * layer_idx)


def _head_gain(q_gain, n_q, k_gain, n_k, n_v):
    scale = HD ** -0.5 * LOG2E
    return jnp.concatenate([jnp.tile(q_gain * scale, n_q // HD), jnp.tile(k_gain, n_k // HD),
                            jnp.ones((n_v,), F32)])


def _encode(x, b, p):
    t = b * SEQ
    x = x.reshape(t, C)
    pos = jnp.arange(SEQ, dtype=F32)

    hn = _head_gain(p['a_q_norm'], C, p['a_k_norm'], C, C)
    qkv = _proj(x, p['n0_attn'], p['a_w_qkv'].astype(BF16), hn, 2 * C,
                rope=_rope_tables(pos, None, HD // 2), rope_half=HD // 2)
    lam_init = _lambda_init(0)
    lam = (jnp.exp(jnp.sum(p['a_lq1'] * p['a_lk1'])) - jnp.exp(jnp.sum(p['a_lq2'] * p['a_lk2']))
           + lam_init).reshape(1)
    o = _diff_attn(qkv.reshape(b, SEQ, 3 * C), lam,
                   (p['a_subln'] * (1.0 - lam_init)).reshape(1, LANES), b)
    x = _out_proj(o.reshape(t, C), p['a_w_o'].astype(BF16), x)
    x = _ffn(x, p['n0_ffn'], p['f0_w_in'], p['f0_conv_w'], p['f0_conv_b'], p['f0_w_out'])

    kvw = GQ_KV * HD
    w = p['b_w_qkv']

    def dup(wc):
        wc = wc.reshape(C, GQ_KV, 1, HD)
        return jnp.broadcast_to(wc, (C, GQ_KV, 2, HD)).reshape(C, 2 * kvw)

    w_dup = jnp.concatenate([w[:, :C], dup(w[:, C:C + kvw]), dup(w[:, C + kvw:])], axis=1)
    hn = _head_gain(p['b_q_norm'], C, p['b_k_norm'], 2 * kvw, 2 * kvw)
    tok = jnp.arange(SEQ)
    rope = _rope_tables((tok // GRID_W).astype(F32), (tok % GRID_W).astype(F32), HD // 4)
    qkv = _proj(x, p['n1_attn'], w_dup.astype(BF16), hn, C + 2 * kvw, rope=rope,
                rope_half=HD // 4)
    o = _gqa(qkv.reshape(b, SEQ, C + 4 * kvw), b)
    x = _out_proj(o.reshape(t, C), p['b_w_o'].astype(BF16), x)
    x = _ffn(x, p['n1_ffn'], p['f1_w_in'], p['f1_conv_w'], p['f1_conv_b'], p['f1_w_out'])

    hn = _head_gain(p['c_q_norm'], C, p['c_k_norm'], C, C)
    qkv = _proj(x, p['n2_attn'], p['c_w_qkv'].astype(BF16), hn, 2 * C)
    o = _na_attn(qkv.reshape(b, SEQ, 3 * C), _na_bias_table(p['c_rel_bias']), b)
    x = _out_proj(o.reshape(t, C), p['c_w_o'].astype(BF16), x)
    x = _ffn(x, p['n2_ffn'], p['f2_w_in'], p['f2_conv_w'], p['f2_conv_b'], p['f2_w_out'])

    r, v, g, kk, kd0, kd1, ag0, ag1, lw0, lw1 = _rwkv_pre(
        x, p['n3_attn'], p['d_mu'], p['d_w_r'], p['d_w_k'], p['d_w_v'], p['d_g1'], p['d_g2'],
        p['d_k_k'], p['d_k_a'], p['d_w0'], p['d_w1'], p['d_w2'], p['d_a0'], p['d_a1'], p['d_a2'])
    y = _wkv(r, v, kk, kd0, kd1, ag0, ag1, lw0, lw1, b)
    x = _rwkv_post(y.reshape(t, C), r, kd0, kd1, v, g, x, p['d_ln_g'], p['d_ln_b'], p['d_r_k'],
                   p['d_w_o'])
    x = _ffn(x, p['n3_ffn'], p['f3_w_in'], p['f3_conv_w'], p['f3_conv_b'], p['f3_w_out'])
    return x.reshape(b, SEQ, C)


def kernel(x_prompt, x_sample,
           n0_attn, n0_ffn, n1_attn, n1_ffn, n2_attn, n2_ffn, n3_attn, n3_ffn, a_w_qkv,
           a_q_norm, a_k_norm, a_lq1, a_lk1, a_lq2, a_lk2, a_subln, a_w_o, b_w_qkv,
           b_q_norm, b_k_norm, b_w_o, c_w_qkv, c_q_norm, c_k_norm, c_rel_bias, c_w_o, d_mu,
           d_w_r, d_w_k, d_w_v, d_w_o, d_g1, d_g2, d_k_k, d_k_a, d_r_k, d_ln_g, d_ln_b,
           d_w0, d_w1, d_w2, d_a0, d_a1, d_a2, f0_w_in, f0_conv_w, f0_conv_b, f0_w_out,
           f1_w_in, f1_conv_w, f1_conv_b, f1_w_out, f2_w_in, f2_conv_w, f2_conv_b, f2_w_out,
           f3_w_in, f3_conv_w, f3_conv_b, f3_w_out):
    p = dict(locals())
    xs = (p.pop('x_prompt'), p.pop('x_sample'))
    return tuple(_encode(x, x.shape[0], p) for x in xs)
```
